```python
import math
import jax
import jax.numpy as jnp
from jax import lax
import numpy as np

D_MODEL = 2048
BATCH = 2
SEQ = 4096
DEPTH = 4
DEC_BATCH = 8
DEC_SEQ = 4
PAST_LEN = 16384
PAGE_SIZE = 128

N_MIXERS = 2
N_A_LAYERS = (DEPTH + 1) // 2
N_B_LAYERS = DEPTH // 2
CHUNK = 128
D_A = D_MODEL
N_GROUPS_A = 8
GROUP_DIM_A = D_A // N_GROUPS_A
N_HEADS = 16
HEAD_DIM = D_MODEL // N_HEADS
ROT_DIM = HEAD_DIM // 4
ROPE_THETA = 500000.0
MOBA_BLOCK = 256
MOBA_TOPK = 3
Q_CHUNK = 16
N_EGROUPS = 4
N_EXP_PER_GROUP = 4
N_EXPERTS = N_EGROUPS * N_EXP_PER_GROUP
D_EXPERT = D_MODEL // 4
TOPK_E = 2
D_PLE = 256
ALPHA = (2.0 * DEPTH) ** 0.25
BETA = (8.0 * DEPTH) ** -0.25
LN_EPS = 1e-5
POOL_NUM = 5
POOL_DEN = 4

kernel_name = 'hybrid_gmlp_moba_hmoe_step'


def layer_norm(x, g, b):
    xf = x.astype(jnp.float32)
    mu = jnp.mean(xf, axis=-1, keepdims=True)
    var = jnp.mean(jnp.square(xf - mu), axis=-1, keepdims=True)
    y = (xf - mu) * lax.rsqrt(var + LN_EPS)
    return (y * g.astype(jnp.float32) + b.astype(jnp.float32)).astype(x.dtype)


def partial_rope(x, pos):
    half = ROT_DIM // 2
    inv = ROPE_THETA ** (-jnp.arange(half, dtype=jnp.float32) / half)
    ang = pos.astype(jnp.float32)[:, None] * inv[None, :]
    cos = jnp.cos(ang)[None, :, None, :]
    sin = jnp.sin(ang)[None, :, None, :]
    xr = x[..., :ROT_DIM].astype(jnp.float32)
    x1, x2 = xr[..., :half], xr[..., half:]
    rot = jnp.concatenate([x1 * cos - x2 * sin, x2 * cos + x1 * sin], axis=-1).astype(x.dtype)
    return jnp.concatenate([rot, x[..., ROT_DIM:]], axis=-1)


def chunk_gmlp_mixer(x, w_in, ln_g, ln_b, w_s, b_s, w_out):
    B, T, _ = x.shape
    L = min(T, CHUNK)
    C = T // L
    z = jax.nn.gelu(x @ w_in)
    u, v = z[..., :D_A], z[..., D_A:]
    v = layer_norm(v, ln_g, ln_b)
    vc = v.reshape(B, C, L, N_GROUPS_A, GROUP_DIM_A)
    causal = jnp.tril(jnp.ones((L, L), dtype=bool))
    ws = jnp.where(causal[None], w_s[:, :L, :L], 0)
    s = jnp.einsum('gts,bcsgd->bctgd', ws, vc) + b_s[:, :L].T[:, :, None]
    out = (u * s.reshape(B, T, D_A)) @ w_out
    return out, v


def qkv_rope(x, w_qkv, pos):
    B, T, _ = x.shape
    qkv = (x @ w_qkv).reshape(B, T, 3, N_HEADS, HEAD_DIM)
    return partial_rope(qkv[:, :, 0], pos), partial_rope(qkv[:, :, 1], pos), qkv[:, :, 2]


def moba_attention(q, k_all, v_all, q_pos):
    B, T, H, Dh = q.shape
    L = k_all.shape[1]
    NB = -(-L // MOBA_BLOCK)
    pad = NB * MOBA_BLOCK - L
    kb = jnp.pad(k_all, ((0, 0), (0, pad), (0, 0), (0, 0))).reshape(B, NB, MOBA_BLOCK, H, Dh)
    vb = jnp.pad(v_all, ((0, 0), (0, pad), (0, 0), (0, 0))).reshape(B, NB, MOBA_BLOCK, H, Dh)
    k_mean = jnp.mean(kb.astype(jnp.float32), axis=2)
    n_sel = min(MOBA_TOPK, NB)
    qc = math.gcd(T, Q_CHUNK)
    nq = T // qc
    q_chunks = q.reshape(B, nq, qc, H, Dh).transpose(1, 0, 3, 2, 4)
    pos_chunks = q_pos.reshape(nq, qc)
    bi = jnp.arange(B)[:, None, None, None]
    hi = jnp.arange(H)[None, :, None, None]
    offs = jnp.arange(MOBA_BLOCK, dtype=jnp.int32)
    scale = HEAD_DIM ** -0.5

    def one_chunk(args):
        qh, pos = args
        cur = pos // MOBA_BLOCK
        gate = jnp.einsum('bhqd,bnhd->bhqn', qh.astype(jnp.float32), k_mean)
        fully_past = jnp.arange(NB)[None, :] < cur[:, None]
        gate = jnp.where(fully_past, gate, -jnp.inf)
        _, top = lax.top_k(gate, n_sel)
        own = jnp.broadcast_to(cur[None, None, :, None], (B, H, qc, 1)).astype(top.dtype)
        blocks = jnp.concatenate([top, own], axis=-1)
        slot_ok = jnp.concatenate([jnp.arange(n_sel)[None, :] < cur[:, None],
                                   jnp.ones((qc, 1), dtype=bool)], axis=-1)
        kg = kb[bi, blocks, :, hi]
        vg = vb[bi, blocks, :, hi]
        s = jnp.einsum('bhqd,bhqjsd->bhqjs', qh, kg,
                       preferred_element_type=jnp.float32) * scale
        kpos = blocks[..., None] * MOBA_BLOCK + offs
        ok = slot_ok[None, None, :, :, None] & (kpos <= pos[None, None, :, None, None])
        s = jnp.where(ok, s, -jnp.inf)
        p = jax.nn.softmax(s.reshape(B, H, qc, -1), axis=-1).reshape(s.shape).astype(vg.dtype)
        return jnp.einsum('bhqjs,bhqjsd->bhqd', p, vg)

    out = lax.map(one_chunk, (q_chunks, pos_chunks))
    return out.transpose(1, 0, 3, 2, 4).reshape(B, T, H, Dh)


def hier_moe(h, w_rc, b_rc, w_rf, b_rf, w_gate, w_up, w_down):
    B, T, D = h.shape
    t = h.reshape(B * T, D)
    pc = jax.nn.softmax((t @ w_rc + b_rc).astype(jnp.float32), axis=-1)
    g = jnp.argmax(pc, axis=-1)
    pg = jnp.max(pc, axis=-1)
    lf = jnp.einsum('nd,dge->nge', t, w_rf) + b_rf
    lf = jnp.take_along_axis(lf, g[:, None, None], axis=1)[:, 0]
    pf = jax.nn.softmax(lf.astype(jnp.float32), axis=-1)
    wk, ik = lax.top_k(pf, TOPK_E)
    wk = wk / jnp.sum(wk, axis=-1, keepdims=True) * pg[:, None]
    eid = g[:, None] * N_EXP_PER_GROUP + ik
    gates = jnp.sum(jax.nn.one_hot(eid, N_EXPERTS, dtype=jnp.float32) * wk[..., None], axis=1)
    a = jnp.einsum('nd,edf->nef', t, w_gate)
    u = jnp.einsum('nd,edf->nef', t, w_up)
    hid = jax.nn.silu(a) * u * gates[:, :, None].astype(t.dtype)
    y = jnp.einsum('nef,efd->nd', hid, w_down)
    return y.reshape(B, T, D)


def finish_layer(x, mix, p, ln_g, ln_b, w_rc, b_rc, w_rf, b_rf, w_gate, w_up, w_down,
                 w_pp, w_pg, b_pg):
    h = layer_norm(ALPHA * x + mix, ln_g[0], ln_b[0])
    h = layer_norm(ALPHA * h + hier_moe(h, w_rc, b_rc, w_rf, b_rf, w_gate, w_up, w_down),
                   ln_g[1], ln_b[1])
    z = jax.nn.sigmoid(h @ w_pg + b_pg) * (p @ w_pp)
    return h + layer_norm(z, ln_g[2], ln_b[2])


def setup_inputs(seed: int = 0) -> dict:
    key = jax.random.key(seed)
    ks = iter(jax.random.split(key, 40))
    f32 = jnp.float32

    def nrm(shape, scale=1.0):
        return jax.random.normal(next(ks), shape, f32) * scale

    n_pages = PAST_LEN // PAGE_SIZE
    n_pool = (DEC_BATCH * n_pages * POOL_NUM) // POOL_DEN
    x_prompt = nrm((BATCH, SEQ, D_MODEL))
    x_sample = nrm((DEC_BATCH, DEC_SEQ, D_MODEL))
    cache_k = nrm((n_pool, PAGE_SIZE, N_B_LAYERS, N_HEADS, HEAD_DIM))
    cache_v = nrm((n_pool, PAGE_SIZE, N_B_LAYERS, N_HEADS, HEAD_DIM))
    page_table = jax.random.permutation(next(ks), n_pool)[: DEC_BATCH * n_pages]
    page_table = page_table.reshape(DEC_BATCH, n_pages).astype(jnp.int32)
    p_prompt = nrm((DEPTH, BATCH, SEQ, D_PLE))
    p_sample = nrm((DEPTH, DEC_BATCH, DEC_SEQ, D_PLE))
    a_w_in = nrm((N_A_LAYERS, D_MODEL, 2 * D_A), D_MODEL ** -0.5)
    a_ln_g = 1.0 + nrm((N_A_LAYERS, D_A), 0.05)
    a_ln_b = nrm((N_A_LAYERS, D_A), 0.02)
    a_w_s = nrm((N_A_LAYERS, N_GROUPS_A, CHUNK, CHUNK), CHUNK ** -0.5)
    a_b_s = 1.0 + nrm((N_A_LAYERS, N_GROUPS_A, CHUNK), 0.1)
    a_w_out = nrm((N_A_LAYERS, D_A, D_MODEL), D_A ** -0.5 * BETA)
    b_w_qkv = nrm((N_B_LAYERS, D_MODEL, 3 * D_MODEL), D_MODEL ** -0.5)
    b_w_o = nrm((N_B_LAYERS, D_MODEL, D_MODEL), D_MODEL ** -0.5 * BETA)
    ln_g = 1.0 + nrm((DEPTH, 3, D_MODEL), 0.05)
    ln_b = nrm((DEPTH, 3, D_MODEL), 0.02)
    moe_w_rc = nrm((DEPTH, D_MODEL, N_EGROUPS), D_MODEL ** -0.5)
    moe_b_rc = nrm((DEPTH, N_EGROUPS), 0.01)
    moe_w_rf = nrm((DEPTH, D_MODEL, N_EGROUPS, N_EXP_PER_GROUP), D_MODEL ** -0.5)
    moe_b_rf = nrm((DEPTH, N_EGROUPS, N_EXP_PER_GROUP), 0.01)
    moe_w_gate = nrm((DEPTH, N_EXPERTS, D_MODEL, D_EXPERT), D_MODEL ** -0.5)
    moe_w_up = nrm((DEPTH, N_EXPERTS, D_MODEL, D_EXPERT), D_MODEL ** -0.5)
    moe_w_down = nrm((DEPTH, N_EXPERTS, D_EXPERT, D_MODEL), D_EXPERT ** -0.5 * BETA)
    ple_w_proj = nrm((DEPTH, D_PLE, D_MODEL), D_PLE ** -0.5)
    ple_w_gate = nrm((DEPTH, D_MODEL, D_MODEL), D_MODEL ** -0.5)
    ple_b_gate = nrm((DEPTH, D_MODEL), 0.02)
    return {'x_prompt': x_prompt, 'x_sample': x_sample, 'cache_k': cache_k, 'cache_v': cache_v,
            'page_table': page_table, 'p_prompt': p_prompt, 'p_sample': p_sample,
            'a_w_in': a_w_in, 'a_ln_g': a_ln_g, 'a_ln_b': a_ln_b, 'a_w_s': a_w_s, 'a_b_s': a_b_s,
            'a_w_out': a_w_out, 'b_w_qkv': b_w_qkv, 'b_w_o': b_w_o, 'ln_g': ln_g, 'ln_b': ln_b,
            'moe_w_rc': moe_w_rc, 'moe_b_rc': moe_b_rc, 'moe_w_rf': moe_w_rf, 'moe_b_rf': moe_b_rf,
            'moe_w_gate': moe_w_gate, 'moe_w_up': moe_w_up, 'moe_w_down': moe_w_down,
            'ple_w_proj': ple_w_proj, 'ple_w_gate': ple_w_gate, 'ple_b_gate': ple_b_gate}


def reference(x_prompt, x_sample, cache_k, cache_v, page_table, p_prompt, p_sample,
              a_w_in, a_ln_g, a_ln_b, a_w_s, a_b_s, a_w_out, b_w_qkv, b_w_o, ln_g, ln_b,
              moe_w_rc, moe_b_rc, moe_w_rf, moe_b_rf, moe_w_gate, moe_w_up, moe_w_down,
              ple_w_proj, ple_w_gate, ple_b_gate):
    Bp, Tp, _ = x_prompt.shape
    Bs, Ts, _ = x_sample.shape
    past_len = page_table.shape[1] * PAGE_SIZE
    pos_p = jnp.arange(Tp, dtype=jnp.int32)
    pos_s = past_len + jnp.arange(Ts, dtype=jnp.int32)
    xp, xs = x_prompt, x_sample
    kp_rows, vp_rows, ks_rows, vs_rows, chunk_rows = [], [], [], [], []
    for i in range(DEPTH):
        j = i // N_MIXERS
        if i % N_MIXERS == 0:
            wa = (a_w_in[j], a_ln_g[j], a_ln_b[j], a_w_s[j], a_b_s[j], a_w_out[j])
            mp, _ = chunk_gmlp_mixer(xp, *wa)
            ms, v_chunk = chunk_gmlp_mixer(xs, *wa)
            chunk_rows.append(v_chunk)
        else:
            qp, kp, vp = qkv_rope(xp, b_w_qkv[j], pos_p)
            mp = moba_attention(qp, kp, vp, pos_p).reshape(Bp, Tp, D_MODEL) @ b_w_o[j]
            qs, ks_new, vs_new = qkv_rope(xs, b_w_qkv[j], pos_s)
            past_k = cache_k[page_table, :, j].reshape(Bs, past_len, N_HEADS, HEAD_DIM)
            past_v = cache_v[page_table, :, j].reshape(Bs, past_len, N_HEADS, HEAD_DIM)
            k_all = jnp.concatenate([past_k.astype(ks_new.dtype), ks_new], axis=1)
            v_all = jnp.concatenate([past_v.astype(vs_new.dtype), vs_new], axis=1)
            ms = moba_attention(qs, k_all, v_all, pos_s).reshape(Bs, Ts, D_MODEL) @ b_w_o[j]
            kp_rows.append(kp)
            vp_rows.append(vp)
            ks_rows.append(ks_new)
            vs_rows.append(vs_new)
        lp = (ln_g[i], ln_b[i], moe_w_rc[i], moe_b_rc[i], moe_w_rf[i], moe_b_rf[i],
              moe_w_gate[i], moe_w_up[i], moe_w_down[i], ple_w_proj[i], ple_w_gate[i], ple_b_gate[i])
        xp = finish_layer(xp, mp, p_prompt[i], *lp)
        xs = finish_layer(xs, ms, p_sample[i], *lp)
    k_prompt_new = jnp.stack(kp_rows, axis=2)
    v_prompt_new = jnp.stack(vp_rows, axis=2)
    k_sample_new = jnp.stack(ks_rows, axis=2)
    v_sample_new = jnp.stack(vs_rows, axis=2)
    chunk_v_sample_new = jnp.stack(chunk_rows, axis=2)
    return (xp, xs, k_prompt_new, v_prompt_new, k_sample_new, v_sample_new, chunk_v_sample_new)
```

```python
import functools

import numpy as np
import jax
import jax.numpy as jnp
from jax import lax
from jax.experimental import pallas as pl
from jax.experimental.pallas import tpu as pltpu

F32 = jnp.float32
BF16 = jnp.bfloat16

LANES = 128
TM = 256
MOBA_BLOCK = 256
MOBA_TOPK = 3
ROPE_THETA = 500000.0
LN_EPS = 1e-5
Q_PAD = 16
VMEM_LIMIT = 56 * 1024 * 1024
NEG_INF = float("-inf")


def _cparams(n_axes):
    return pltpu.CompilerParams(dimension_semantics=("arbitrary",) * n_axes,
                                vmem_limit_bytes=VMEM_LIMIT)


def _const_spec(shape, n_prefetch=0):
    zeros = (0,) * len(shape)
    if n_prefetch == 0:
        imap = lambda *a: zeros
    else:
        imap = lambda *a: zeros
    return pl.BlockSpec(shape, imap, pipeline_mode=pl.Buffered(1))


def _aligned(x, m):
    return x if isinstance(x, int) else pl.multiple_of(x, m)


def _ln(x, g, b):
    mu = jnp.mean(x, axis=-1, keepdims=True)
    xc = x - mu
    var = jnp.mean(xc * xc, axis=-1, keepdims=True)
    return xc * lax.rsqrt(var + LN_EPS) * g + b


def _to_slabs(ref, val):
    rows, d = val.shape
    s_n = d // LANES
    for s in range(s_n):
        ref[pl.ds(s, rows, stride=s_n), :] = val[:, s * LANES:(s + 1) * LANES]


def _from_slabs(ref, base, rows, s_n):
    cols = [ref[pl.ds(base + s, rows, stride=s_n), :] for s in range(s_n)]
    return jnp.concatenate(cols, axis=1)


def _route(h, wr_ref, br_ref, n_grp, epg):
    logits = jnp.dot(h.astype(BF16), wr_ref[...], preferred_element_type=F32) + br_ref[...]
    lane = lax.broadcasted_iota(jnp.int32, logits.shape, 1).astype(F32)
    big = float(LANES)
    lc = jnp.where(lane < n_grp, logits, NEG_INF)
    mc = jnp.max(lc, axis=1, keepdims=True)
    zc = jnp.sum(jnp.exp(lc - mc), axis=1, keepdims=True)
    pg = 1.0 / zc
    g = jnp.min(jnp.where(lc == mc, lane, big), axis=1, keepdims=True)
    lo = n_grp + g * epg
    lf = jnp.where((lane >= lo) & (lane < lo + epg), logits, NEG_INF)
    m1 = jnp.max(lf, axis=1, keepdims=True)
    i1 = jnp.min(jnp.where(lf == m1, lane, big), axis=1, keepdims=True)
    lf2 = jnp.where(lane == i1, NEG_INF, lf)
    m2 = jnp.max(lf2, axis=1, keepdims=True)
    i2 = jnp.min(jnp.where(lf2 == m2, lane, big), axis=1, keepdims=True)
    e2 = jnp.exp(m2 - m1)
    w1 = pg / (1.0 + e2)
    w2 = pg * e2 / (1.0 + e2)
    out = jnp.where(lane == 0, i1 - n_grp, 0.0)
    out = jnp.where(lane == 1, i2 - n_grp, out)
    out = jnp.where(lane == 2, w1, out)
    out = jnp.where(lane == 3, w2, out)
    return out


def _post_mix(x, mix, g_ref, b_ref, wr_ref, br_ref, h_ref, hg_ref, route_ref, alpha, n_grp, epg):
    h = _ln(alpha * x + mix, g_ref[...], b_ref[...])
    h_ref[...] = h
    _to_slabs(hg_ref, h)
    route_ref[...] = _route(h, wr_ref, br_ref, n_grp, epg)


def _gmlp_kernel(x_ref, win_ref, ag_ref, ab_ref, ws_ref, msk_ref, bs_ref, wout_ref,
                 g_ref, b_ref, wr_ref, br_ref,
                 h_ref, hg_ref, route_ref, vn_ref, gated_ref,
                 *, n_groups, chunk, alpha, n_grp, epg):
    x = x_ref[...]
    z = jnp.dot(x.astype(BF16), win_ref[...], preferred_element_type=F32)
    z = jax.nn.gelu(z)
    da = z.shape[1] // 2
    u = z[:, :da]
    vn = _ln(z[:, da:], ag_ref[...], ab_ref[...])

    @pl.when(pl.program_id(0) == pl.num_programs(0) - 1)
    def _():
        vn_ref[...] = vn

    vnb = vn.astype(BF16)
    gd = da // n_groups
    mask = msk_ref[0] > 0.0
    for g in range(n_groups):
        wsg = jnp.where(mask, ws_ref[0, g], 0.0).astype(BF16)
        for c in range(x.shape[0] // chunk):
            r0 = c * chunk
            s = jnp.dot(wsg, vnb[r0:r0 + chunk, g * gd:(g + 1) * gd], preferred_element_type=F32)
            s = s + bs_ref[0][:, g * gd:(g + 1) * gd]
            gated_ref[r0:r0 + chunk, g * gd:(g + 1) * gd] = (u[r0:r0 + chunk, g * gd:(g + 1) * gd] * s).astype(BF16)
    mix = jnp.dot(gated_ref[...], wout_ref[...], preferred_element_type=F32)
    _post_mix(x, mix, g_ref, b_ref, wr_ref, br_ref, h_ref, hg_ref, route_ref, alpha, n_grp, epg)


def _gmlp_layer(x, w_in, a_g, a_b, ws2, msk2, bs2, w_out, g0, b0, wr, br, *, alpha, n_grp, epg):
    nt, d = x.shape
    da = w_out.shape[0]
    n_groups, chunk = ws2.shape[1], ws2.shape[2]
    n_tiles = nt // TM
    last = n_tiles - 1
    s_n = d // LANES
    kern = functools.partial(_gmlp_kernel, n_groups=n_groups, chunk=chunk, alpha=alpha, n_grp=n_grp, epg=epg)
    sel = lambda i: (i // last, 0, 0, 0) if last > 0 else (0, 0, 0, 0)
    sel3 = lambda i: (i // last, 0, 0) if last > 0 else (0, 0, 0)
    return pl.pallas_call(
        kern,
        grid=(n_tiles,),
        in_specs=[
            pl.BlockSpec((TM, d), lambda i: (i, 0)),
            _const_spec((d, 2 * da)),
            _const_spec((1, da)), _const_spec((1, da)),
            pl.BlockSpec((1, n_groups, chunk, chunk), sel),
            pl.BlockSpec((1, chunk, chunk), sel3),
            pl.BlockSpec((1, chunk, da), sel3),
            _const_spec((da, d)),
            _const_spec((1, d)), _const_spec((1, d)),
            _const_spec((d, LANES)), _const_spec((1, LANES)),
        ],
        out_specs=[
            pl.BlockSpec((TM, d), lambda i: (i, 0)),
            pl.BlockSpec((TM * s_n, LANES), lambda i: (i, 0)),
            pl.BlockSpec((TM, LANES), lambda i: (i, 0)),
            pl.BlockSpec((TM, da), lambda i: (0, 0)),
        ],
        out_shape=[
            jax.ShapeDtypeStruct((nt, d), F32),
            jax.ShapeDtypeStruct((nt * s_n, LANES), F32),
            jax.ShapeDtypeStruct((nt, LANES), F32),
            jax.ShapeDtypeStruct((TM, da), F32),
        ],
        scratch_shapes=[pltpu.VMEM((TM, da), BF16)],
        compiler_params=_cparams(1),
        name="gmlp_mixer",
    )(x, w_in, a_g, a_b, ws2, msk2, bs2, w_out, g0, b0, wr, br)


def _proj_kernel(*refs, rope, want_f32, want_mean, n_heads, rot):
    x_ref, w_ref = refs[0], refs[1]
    pos = 2
    if rope:
        cos_ref, sa_ref, sb_ref = refs[2:5]
        pos = 5
    outs = list(refs[pos:])
    y = jnp.dot(x_ref[...].astype(BF16), w_ref[...], preferred_element_type=F32)
    if rope:
        hd = y.shape[1] // n_heads
        half = rot // 2
        cos, sa, sb = cos_ref[...], sa_ref[...], sb_ref[...]
        parts = []
        for h in range(n_heads):
            yh = y[:, h * hd:(h + 1) * hd]
            parts.append(yh * cos + pltpu.roll(yh, hd - half, 1) * sa + pltpu.roll(yh, half, 1) * sb)
        y = jnp.concatenate(parts, axis=1)
    if want_f32:
        outs.pop(0)[...] = y
    outs.pop(0)[...] = y.astype(BF16)
    if want_mean:
        outs.pop(0)[0] = jnp.mean(y, axis=0, keepdims=True)


def _proj(x, w, tables, *, rope, want_f32, want_mean, n_heads, rot, name):
    nt, d = x.shape
    dn = w.shape[1]
    n_tiles = nt // TM
    kern = functools.partial(_proj_kernel, rope=rope, want_f32=want_f32, want_mean=want_mean,
                             n_heads=n_heads, rot=rot)
    in_specs = [pl.BlockSpec((TM, d), lambda i: (i, 0)), _const_spec((d, dn))]
    args = [x, w]
    if rope:
        in_specs += [pl.BlockSpec((TM, LANES), lambda i: (i, 0))] * 3
        args += list(tables)
    out_specs, out_shape = [], []
    if want_f32:
        out_specs.append(pl.BlockSpec((TM, dn), lambda i: (i, 0)))
        out_shape.append(jax.ShapeDtypeStruct((nt, dn), F32))
    out_specs.append(pl.BlockSpec((TM, dn), lambda i: (i, 0)))
    out_shape.append(jax.ShapeDtypeStruct((nt, dn), BF16))
    if want_mean:
        out_specs.append(pl.BlockSpec((1, 1, dn), lambda i: (i, 0, 0)))
        out_shape.append(jax.ShapeDtypeStruct((n_tiles, 1, dn), F32))
    return pl.pallas_call(
        kern, grid=(n_tiles,), in_specs=in_specs, out_specs=out_specs, out_shape=out_shape,
        compiler_params=_cparams(1), name=name,
    )(*args)


def _top_blocks(gate, lane, topk):
    big = float(gate.shape[1])
    ids = []
    for _ in range(topk):
        m = jnp.max(gate, axis=1, keepdims=True)
        idx = jnp.min(jnp.where(gate == m, lane, big), axis=1, keepdims=True)
        idx = jnp.where(m > NEG_INF, idx, -1.0)
        ids.append(idx)
        gate = jnp.where(lane == idx, NEG_INF, gate)
    return ids


def _moba_prompt_kernel(q_ref, k_ref, v_ref, km_ref, o_ref, *, blk, topk, scale):
    n_blk = q_ref.shape[0] // blk
    kmb = km_ref[...].astype(BF16)
    nt_dims = (((1,), (1,)), ((), ()))
    row = lax.broadcasted_iota(jnp.int32, (blk, blk), 0)
    col = lax.broadcasted_iota(jnp.int32, (blk, blk), 1)
    causal = col <= row
    lane = lax.broadcasted_iota(jnp.int32, (blk, n_blk), 1).astype(F32)

    def q_tile(qt, carry):
        q0 = _aligned(qt * blk, blk)
        q = q_ref[pl.ds(q0, blk), :]
        gate = lax.dot_general(q, kmb, nt_dims, preferred_element_type=F32)
        gate = jnp.where(lane < jnp.asarray(qt, F32), gate, NEG_INF)
        ids = _top_blocks(gate, lane, topk)

        s = lax.dot_general(q, k_ref[pl.ds(q0, blk), :], nt_dims, preferred_element_type=F32) * scale
        s = jnp.where(causal, s, NEG_INF)
        m0 = jnp.max(s, axis=1, keepdims=True)
        p = jnp.exp(s - m0)
        l0 = jnp.sum(p, axis=1, keepdims=True)
        acc0 = jnp.dot(p.astype(BF16), v_ref[pl.ds(q0, blk), :], preferred_element_type=F32)

        def kv_step(n, c):
            m, l, acc = c
            k0 = _aligned(n * blk, blk)
            s = lax.dot_general(q, k_ref[pl.ds(k0, blk), :], nt_dims, preferred_element_type=F32) * scale
            nf = jnp.asarray(n, F32)
            picked = ids[0] == nf
            for t in range(1, topk):
                picked = picked | (ids[t] == nf)
            s = jnp.where(picked, s, NEG_INF)
            m_new = jnp.maximum(m, jnp.max(s, axis=1, keepdims=True))
            a = jnp.exp(m - m_new)
            p = jnp.exp(s - m_new)
            l = a * l + jnp.sum(p, axis=1, keepdims=True)
            acc = a * acc + jnp.dot(p.astype(BF16), v_ref[pl.ds(k0, blk), :], preferred_element_type=F32)
            return m_new, l, acc

        m, l, acc = lax.fori_loop(0, qt, kv_step, (m0, l0, acc0))
        o_ref[pl.ds(q0, blk), :] = (acc / l).astype(o_ref.dtype)
        return carry

    lax.fori_loop(0, n_blk, q_tile, 0)


def _moba_prompt(q, k, v, km2, *, bp, tp, n_heads, hd):
    nt, d = q.shape
    n_blk = tp // MOBA_BLOCK
    kern = functools.partial(_moba_prompt_kernel, blk=MOBA_BLOCK, topk=MOBA_TOPK, scale=hd ** -0.5)
    seq_spec = pl.BlockSpec((tp, hd), lambda b, h: (b, h))
    return pl.pallas_call(
        kern, grid=(bp, n_heads),
        in_specs=[seq_spec, seq_spec, seq_spec, pl.BlockSpec((n_blk, hd), lambda b, h: (b, h))],
        out_specs=seq_spec,
        out_shape=jax.ShapeDtypeStruct((bp * tp, d), BF16),
        compiler_params=_cparams(2), name="moba_prompt",
    )(q, k, v, km2)


def _cache_mean_kernel(pt_ref, *refs, ppb, inv):
    o_ref = refs[ppb]
    acc = jnp.sum(refs[0][...], axis=0)
    for t in range(1, ppb):
        acc = acc + jnp.sum(refs[t][...], axis=0)
    o_ref[0, 0] = acc * inv


def _cache_means(cache_k, pt_flat, *, bs, n_pages):
    _, page, nbl, n_heads, hd = cache_k.shape
    ppb = MOBA_BLOCK // page
    n_cblk = n_pages // ppb
    kern = functools.partial(_cache_mean_kernel, ppb=ppb, inv=1.0 / MOBA_BLOCK)

    def page_spec(t):
        return pl.BlockSpec((None, page, nbl, n_heads, hd),
                            lambda b, n, pt: (pt[b * n_pages + n * ppb + t], 0, 0, 0, 0))

    return pl.pallas_call(
        kern,
        grid_spec=pltpu.PrefetchScalarGridSpec(
            num_scalar_prefetch=1, grid=(bs, n_cblk),
            in_specs=[page_spec(t) for t in range(ppb)],
            out_specs=pl.BlockSpec((1, 1, nbl, n_heads, hd), lambda b, n, pt: (b, n, 0, 0, 0)),
        ),
        out_shape=jax.ShapeDtypeStruct((bs, n_cblk, nbl, n_heads, hd), F32),
        compiler_params=_cparams(2), name="cache_block_means",
    )(pt_flat, *([cache_k] * ppb))


def _sample_gate_kernel(q_ref, km_ref, o_ref, *, topk):
    n_heads = q_ref.shape[1]
    n_cblk = km_ref.shape[2]
    nt_dims = (((1,), (1,)), ((), ()))
    lane = lax.broadcasted_iota(jnp.int32, (Q_PAD, n_cblk), 1).astype(F32)
    out_lane = lax.broadcasted_iota(jnp.int32, (Q_PAD, LANES), 1)
    for h in range(n_heads):
        gate = lax.dot_general(q_ref[0, h], km_ref[0, h].astype(BF16), nt_dims, preferred_element_type=F32)
        ids = _top_blocks(gate, lane, topk)
        out = jnp.zeros((Q_PAD, LANES), F32)
        for t in range(topk):
            out = jnp.where(out_lane == t, ids[t], out)
        o_ref[0, h] = out.astype(jnp.int32)


def _sample_gate(q_s, km_s):
    bs, n_heads, _, hd = q_s.shape
    n_cblk = km_s.shape[2]
    kern = functools.partial(_sample_gate_kernel, topk=MOBA_TOPK)
    return pl.pallas_call(
        kern, grid=(bs,),
        in_specs=[pl.BlockSpec((1, n_heads, Q_PAD, hd), lambda b: (b, 0, 0, 0)),
                  pl.BlockSpec((1, n_heads, n_cblk, hd), lambda b: (b, 0, 0, 0))],
        out_specs=pl.BlockSpec((1, n_heads, Q_PAD, LANES), lambda b: (b, 0, 0, 0)),
        out_shape=jax.ShapeDtypeStruct((bs, n_heads, Q_PAD, LANES), jnp.int32),
        compiler_params=_cparams(1), name="sample_gate",
    )(q_s, km_s)


def _sample_attn_kernel(pt_ref, ids_ref, ck_hbm, cv_hbm, q_ref, kn_ref, vn_ref, o_ref, kbuf, vbuf, sem,
                        *, layer, ts, topk, ppb, n_pages, scale):
    b, h = pl.program_id(0), pl.program_id(1)
    n_heads = pl.num_programs(1)
    step = b * n_heads + h
    n_steps = pl.num_programs(0) * n_heads
    page = kbuf.shape[2]
    n_sel = ts * topk * ppb

    def copies(bb, hh, slot):
        out = []
        for qi in range(ts):
            for t in range(topk):
                blk = ids_ref[((bb * n_heads + hh) * ts + qi) * topk + t]
                for half in range(ppb):
                    pg = pt_ref[bb * n_pages + blk * ppb + half]
                    j = (qi * topk + t) * ppb + half
                    out.append(pltpu.make_async_copy(ck_hbm.at[pg, :, layer, hh, :], kbuf.at[slot, j], sem.at[0, slot]))
                    out.append(pltpu.make_async_copy(cv_hbm.at[pg, :, layer, hh, :], vbuf.at[slot, j], sem.at[1, slot]))
        return out

    @pl.when(step == 0)
    def _():
        for c in copies(b, h, 0):
            c.start()

    @pl.when(step + 1 < n_steps)
    def _():
        nxt = step + 1
        for c in copies(nxt // n_heads, nxt % n_heads, nxt % 2):
            c.start()

    slot = step % 2
    for c in copies(b, h, slot):
        c.wait()

    nt_dims = (((1,), (1,)), ((), ()))
    q = q_ref[0, 0]
    rows = lax.broadcasted_iota(jnp.int32, (Q_PAD, Q_PAD), 0)
    cols = lax.broadcasted_iota(jnp.int32, (Q_PAD, Q_PAD), 1)
    s_new = lax.dot_general(q, kn_ref[0, 0], nt_dims, preferred_element_type=F32) * scale
    s_new = jnp.where((cols <= rows) & (cols < ts), s_new, NEG_INF)
    out_rows = lax.broadcasted_iota(jnp.int32, (Q_PAD, q.shape[1]), 0)
    out = jnp.zeros((Q_PAD, q.shape[1]), F32)
    per_q = topk * ppb
    for qi in range(ts):
        kq = kbuf[slot, qi * per_q:(qi + 1) * per_q].reshape(per_q * page, -1).astype(BF16)
        vq = vbuf[slot, qi * per_q:(qi + 1) * per_q].reshape(per_q * page, -1).astype(BF16)
        s = lax.dot_general(q, kq, nt_dims, preferred_element_type=F32) * scale
        m = jnp.maximum(jnp.max(s, axis=1, keepdims=True), jnp.max(s_new, axis=1, keepdims=True))
        p = jnp.exp(s - m)
        pn = jnp.exp(s_new - m)
        l = jnp.sum(p, axis=1, keepdims=True) + jnp.sum(pn, axis=1, keepdims=True)
        o = jnp.dot(p.astype(BF16), vq, preferred_element_type=F32)
        o = o + jnp.dot(pn.astype(BF16), vn_ref[0, 0], preferred_element_type=F32)
        out = jnp.where(out_rows == qi, o / l, out)
    o_ref[0, 0] = out


def _sample_attn(pt_flat, ids_flat, cache_k, cache_v, q_s, kn_s, vn_s, *, layer, ts, n_pages):
    bs, n_heads, _, hd = q_s.shape
    page = cache_k.shape[1]
    ppb = MOBA_BLOCK // page
    n_sel = ts * MOBA_TOPK * ppb
    kern = functools.partial(_sample_attn_kernel, layer=layer, ts=ts, topk=MOBA_TOPK, ppb=ppb,
                             n_pages=n_pages, scale=hd ** -0.5)
    qspec = pl.BlockSpec((1, 1, Q_PAD, hd), lambda b, h, pt, ids: (b, h, 0, 0))
    return pl.pallas_call(
        kern,
        grid_spec=pltpu.PrefetchScalarGridSpec(
            num_scalar_prefetch=2, grid=(bs, n_heads),
            in_specs=[pl.BlockSpec(memory_space=pl.ANY), pl.BlockSpec(memory_space=pl.ANY), qspec, qspec, qspec],
            out_specs=qspec,
            scratch_shapes=[pltpu.VMEM((2, n_sel, page, hd), F32), pltpu.VMEM((2, n_sel, page, hd), F32),
                            pltpu.SemaphoreType.DMA((2, 2))],
        ),
        out_shape=jax.ShapeDtypeStruct((bs, n_heads, Q_PAD, hd), F32),
        compiler_params=_cparams(2), name="sample_attn",
    )(pt_flat, ids_flat, cache_k, cache_v, q_s, kn_s, vn_s)


def _oproj_kernel(a_ref, tail_ref, x_ref, wo_ref, g_ref, b_ref, wr_ref, br_ref, h_ref, hg_ref, route_ref,
                  *, alpha, n_grp, epg):
    is_tail = pl.program_id(0) == pl.num_programs(0) - 1
    a = jnp.where(is_tail, tail_ref[...], a_ref[...])
    mix = jnp.dot(a, wo_ref[...], preferred_element_type=F32)
    _post_mix(x_ref[...], mix, g_ref, b_ref, wr_ref, br_ref, h_ref, hg_ref, route_ref, alpha, n_grp, epg)


def _oproj_layer(attn, tail, x, w_o, g0, b0, wr, br, *, alpha, n_grp, epg):
    nt, d = x.shape
    s_n = d // LANES
    last_prompt = attn.shape[0] // TM - 1
    kern = functools.partial(_oproj_kernel, alpha=alpha, n_grp=n_grp, epg=epg)
    return pl.pallas_call(
        kern, grid=(nt // TM,),
        in_specs=[pl.BlockSpec((TM, d), lambda i: (jnp.minimum(i, last_prompt), 0)),
                  pl.BlockSpec((TM, d), lambda i: (0, 0)),
                  pl.BlockSpec((TM, d), lambda i: (i, 0)),
                  _const_spec((d, d)), _const_spec((1, d)), _const_spec((1, d)),
                  _const_spec((d, LANES)), _const_spec((1, LANES))],
        out_specs=[pl.BlockSpec((TM, d), lambda i: (i, 0)),
                   pl.BlockSpec((TM * s_n, LANES), lambda i: (i, 0)),
                   pl.BlockSpec((TM, LANES), lambda i: (i, 0))],
        out_shape=[jax.ShapeDtypeStruct((nt, d), F32),
                   jax.ShapeDtypeStruct((nt * s_n, LANES), F32),
                   jax.ShapeDtypeStruct((nt, LANES), F32)],
        compiler_params=_cparams(1), name="attn_out_proj",
    )(attn, tail, x, w_o, g0, b0, wr, br)


GATHER_UNROLL = 16


def _start_row_gather(idx_ref, idx_base, n_rows, src_hbm, dst, dst_base, s_n, sem):
    def body(rr, c):
        for k in range(GATHER_UNROLL):
            r = rr * GATHER_UNROLL + k
            row = idx_ref[idx_base + r]
            d0 = _aligned((dst_base + r) * s_n, s_n)
            pltpu.make_async_copy(src_hbm.at[row], dst.at[pl.ds(d0, s_n)], sem).start()
        return c
    lax.fori_loop(0, n_rows // GATHER_UNROLL, body, 0)


def _moe_kernel(te_ref, rt_ref, nu_ref, hg_hbm, rw_ref, wg_ref, wu_ref, wd_ref, y_ref, xbuf, sem):
    i = pl.program_id(0)
    n_used = nu_ref[0]
    s_n = hg_hbm.shape[1]
    rows = rw_ref.shape[0]

    @pl.when(i == 0)
    def _():
        _start_row_gather(rt_ref, 0, rows, hg_hbm, xbuf.at[0], 0, s_n, sem.at[0])

    @pl.when(i + 1 < n_used)
    def _():
        nxt = i + 1
        _start_row_gather(rt_ref, nxt * rows, rows, hg_hbm, xbuf.at[nxt % 2], 0, s_n, sem.at[nxt % 2])

    @pl.when(i < n_used)
    def _():
        slot = i % 2
        pltpu.make_async_copy(xbuf.at[slot], xbuf.at[slot], sem.at[slot]).wait()
        x = _from_slabs(xbuf.at[slot], 0, rows, s_n).astype(BF16)
        a = jnp.dot(x, wg_ref[0], preferred_element_type=F32)
        u = jnp.dot(x, wu_ref[0], preferred_element_type=F32)
        hid = (a * jax.nn.sigmoid(a)) * u * rw_ref[...]
        y = jnp.dot(hid.astype(BF16), wd_ref[0], preferred_element_type=F32)
        _to_slabs(y_ref, y)

    @pl.when(i >= n_used)
    def _():
        y_ref[...] = jnp.zeros_like(y_ref)


def _moe_plan(route, n_exp):
    nt = route.shape[0]
    eid = route[:, :2].astype(jnp.int32)
    e_flat = eid.T.reshape(-1)
    w_flat = route[:, 2:4].T.reshape(-1)
    ar = jnp.arange(n_exp, dtype=jnp.int32)
    onehot = (e_flat[:, None] == ar[None, :]).astype(jnp.int32)
    csum = jnp.cumsum(onehot, axis=0)
    counts = csum[-1]
    rank = jnp.take_along_axis(csum, e_flat[:, None], axis=1)[:, 0] - 1
    padded = ((counts + TM - 1) // TM) * TM
    pend = jnp.cumsum(padded)
    pstart = pend - padded
    start = jnp.cumsum(counts) - counts
    pos = pstart[e_flat] + rank
    n_rows = 2 * nt + n_exp * TM
    n_tiles = n_rows // TM
    tile_start = jnp.arange(n_tiles, dtype=jnp.int32) * TM
    tile_e = jnp.sum((tile_start[:, None] >= pend[None, :]).astype(jnp.int32), axis=1)
    last_e = jnp.max(jnp.where(counts > 0, ar, 0))
    tile_e = jnp.minimum(tile_e, last_e)
    n_used = (pend[-1] // TM).astype(jnp.int32).reshape(1)
    order = jnp.argsort(e_flat, stable=True).astype(jnp.int32)
    p = jnp.arange(n_rows, dtype=jnp.int32)
    pe = jnp.repeat(tile_e, TM)
    off = p - pstart[pe]
    valid = (off >= 0) & (off < counts[pe])
    src = order[jnp.clip(start[pe] + off, 0, 2 * nt - 1)]
    row_token = jnp.where(valid, src % nt, 0).astype(jnp.int32)
    row_w = jnp.where(valid, w_flat[src], 0.0).reshape(n_rows, 1)
    return tile_e.astype(jnp.int32), row_token, n_used, row_w, pos.astype(jnp.int32)


def _moe(hg3, plan, w_gate, w_up, w_down):
    tile_e, row_token, n_used, row_w, _ = plan
    n_rows = row_token.shape[0]
    _, s_n, _ = hg3.shape
    d = s_n * LANES
    f = w_gate.shape[2]
    return pl.pallas_call(
        _moe_kernel,
        grid_spec=pltpu.PrefetchScalarGridSpec(
            num_scalar_prefetch=3, grid=(n_rows // TM,),
            in_specs=[pl.BlockSpec(memory_space=pl.ANY),
                      pl.BlockSpec((TM, 1), lambda i, te, rt, nu: (i, 0)),
                      pl.BlockSpec((1, d, f), lambda i, te, rt, nu: (te[i], 0, 0)),
                      pl.BlockSpec((1, d, f), lambda i, te, rt, nu: (te[i], 0, 0)),
                      pl.BlockSpec((1, f, d), lambda i, te, rt, nu: (te[i], 0, 0))],
            out_specs=pl.BlockSpec((TM * s_n, LANES), lambda i, te, rt, nu: (i, 0)),
            scratch_shapes=[pltpu.VMEM((2, TM * s_n, LANES), F32), pltpu.SemaphoreType.DMA((2,))],
        ),
        out_shape=jax.ShapeDtypeStruct((n_rows * s_n, LANES), F32),
        compiler_params=_cparams(1), name="moe_experts",
    )(tile_e, row_token, n_used, hg3, row_w, w_gate, w_up, w_down)


def _finish_kernel(pos_ref, h_ref, ys_hbm, p_ref, wpg_ref, bpg_ref, wpp_ref, g1_ref, b1_ref, g2_ref, b2_ref,
                   o_ref, ybuf, sem, *, alpha, nt):
    i = pl.program_id(0)
    n = pl.num_programs(0)
    s_n = ys_hbm.shape[1]
    rows = h_ref.shape[0]

    def gather(tile, slot):
        for k in range(2):
            _start_row_gather(pos_ref, k * nt + tile * rows, rows, ys_hbm, ybuf.at[slot], k * rows, s_n, sem.at[slot])

    @pl.when(i == 0)
    def _():
        gather(0, 0)

    @pl.when(i + 1 < n)
    def _():
        gather(i + 1, (i + 1) % 2)

    slot = i % 2
    pltpu.make_async_copy(ybuf.at[slot], ybuf.at[slot], sem.at[slot]).wait()
    moe = _from_slabs(ybuf.at[slot], 0, rows, s_n) + _from_slabs(ybuf.at[slot], rows * s_n, rows, s_n)
    h2 = _ln(alpha * h_ref[...] + moe, g1_ref[...], b1_ref[...])
    zg = jnp.dot(h2.astype(BF16), wpg_ref[...], preferred_element_type=F32) + bpg_ref[...]
    zp = jnp.dot(p_ref[...].astype(BF16), wpp_ref[...], preferred_element_type=F32)
    z = jax.nn.sigmoid(zg) * zp
    o_ref[...] = h2 + _ln(z, g2_ref[...], b2_ref[...])


def _finish(pos, h, ys3, p, w_pg, b_pg, w_pp, g1, b1, g2, b2, *, alpha):
    nt, d = h.shape
    dp = p.shape[1]
    s_n = d // LANES
    kern = functools.partial(_finish_kernel, alpha=alpha, nt=nt)
    tile = lambda i, pos: (i, 0)
    const = lambda shape: pl.BlockSpec(shape, lambda i, pos: (0,) * len(shape), pipeline_mode=pl.Buffered(1))
    return pl.pallas_call(
        kern,
        grid_spec=pltpu.PrefetchScalarGridSpec(
            num_scalar_prefetch=1, grid=(nt // TM,),
            in_specs=[pl.BlockSpec((TM, d), tile), pl.BlockSpec(memory_space=pl.ANY),
                      pl.BlockSpec((TM, dp), tile), const((d, d)), const((1, d)), const((dp, d)),
                      const((1, d)), const((1, d)), const((1, d)), const((1, d))],
            out_specs=pl.BlockSpec((TM, d), tile),
            scratch_shapes=[pltpu.VMEM((2, 2 * TM * s_n, LANES), F32), pltpu.SemaphoreType.DMA((2,))],
        ),
        out_shape=jax.ShapeDtypeStruct((nt, d), F32),
        compiler_params=_cparams(1), name="layer_finish",
    )(pos, h, ys3, p, w_pg, b_pg, w_pp, g1, b1, g2, b2)


def _rope_tables(positions, hd, rot):
    half = rot // 2
    inv = ROPE_THETA ** (-jnp.arange(half, dtype=F32) / half)
    ang = positions.astype(F32)[:, None] * inv[None, :]
    cos, sin = jnp.cos(ang), jnp.sin(ang)
    n = positions.shape[0]
    ones = jnp.ones((n, hd - rot), F32)
    zeros_h = jnp.zeros((n, half), F32)
    zeros_r = jnp.zeros((n, hd - rot), F32)
    cos_t = jnp.concatenate([cos, cos, ones], axis=1)
    sa = jnp.concatenate([-sin, zeros_h, zeros_r], axis=1)
    sb = jnp.concatenate([zeros_h, sin, zeros_r], axis=1)
    return cos_t, sa, sb


def kernel(x_prompt, x_sample, cache_k, cache_v, page_table, p_prompt, p_sample, a_w_in, a_ln_g, a_ln_b, a_w_s, a_b_s, a_w_out, b_w_qkv, b_w_o, ln_g, ln_b, moe_w_rc, moe_b_rc, moe_w_rf, moe_b_rf, moe_w_gate, moe_w_up, moe_w_down, ple_w_proj, ple_w_gate, ple_b_gate):
    bp, tp, d = x_prompt.shape
    bs, ts, _ = x_sample.shape
    depth = ln_g.shape[0]
    _, page, nbl, n_heads, hd = cache_k.shape
    n_pages = page_table.shape[1]
    past_len = n_pages * page
    n_groups, chunk = a_w_s.shape[1], a_w_s.shape[2]
    da = a_w_out.shape[1]
    n_grp, epg = moe_w_rf.shape[2], moe_w_rf.shape[3]
    n_exp = n_grp * epg
    rot = hd // 4
    alpha = (2.0 * depth) ** 0.25
    np_rows, ns_rows = bp * tp, bs * ts
    nt = np_rows + TM
    s_n = d // LANES

    assert tp % MOBA_BLOCK == 0 and past_len % MOBA_BLOCK == 0 and MOBA_BLOCK % page == 0
    assert ts <= min(chunk, Q_PAD) and chunk % ts == 0 and ns_rows <= chunk and TM % chunk == 0
    assert n_heads * hd == d and d % LANES == 0 and n_grp + n_exp <= LANES and n_pages // (MOBA_BLOCK // page) >= MOBA_TOPK

    def stream(prompt, sample):
        width = prompt.shape[-1]
        pad = jnp.zeros((TM - ns_rows, width), prompt.dtype)
        return jnp.concatenate([prompt.reshape(np_rows, width), sample.reshape(ns_rows, width), pad], axis=0)

    x = stream(x_prompt, x_sample)
    pt_flat = page_table.reshape(-1).astype(jnp.int32)

    positions = jnp.concatenate([jnp.tile(jnp.arange(tp, dtype=jnp.int32), bp),
                                 jnp.tile(past_len + jnp.arange(ts, dtype=jnp.int32), bs),
                                 jnp.zeros((TM - ns_rows,), jnp.int32)])
    tables = _rope_tables(positions, hd, rot)

    r = np.arange(chunk)
    tril = (r[None, :] <= r[:, None])
    same_seq = (r[None, :] // ts == r[:, None] // ts) & (r[:, None] < ns_rows) & (r[None, :] < ns_rows)
    msk2 = jnp.asarray(np.stack([tril, tril & same_seq]).astype(np.float32))

    km_cache = _cache_means(cache_k, pt_flat, bs=bs, n_pages=n_pages)

    k_rows, v_rows, chunk_rows = [], [], []
    for i in range(depth):
        j = i // 2
        wr = jnp.concatenate([moe_w_rc[i], moe_w_rf[i].reshape(d, n_exp),
                              jnp.zeros((d, LANES - n_grp - n_exp), F32)], axis=1).astype(BF16)
        br = jnp.concatenate([moe_b_rc[i], moe_b_rf[i].reshape(n_exp),
                              jnp.zeros((LANES - n_grp - n_exp,), F32)]).reshape(1, LANES)
        g0, b0 = ln_g[i, 0].reshape(1, d), ln_b[i, 0].reshape(1, d)
        if i % 2 == 0:
            ws_s = jnp.tile(a_w_s[j][:, :ts, :ts], (1, chunk // ts, chunk // ts))
            ws2 = jnp.stack([a_w_s[j], ws_s])
            bs_p = jnp.repeat(a_b_s[j].T, da // n_groups, axis=1)
            bs_s = jnp.repeat(jnp.tile(a_b_s[j][:, :ts], (1, chunk // ts)).T, da // n_groups, axis=1)
            bs2 = jnp.stack([bs_p, bs_s])
            h, hg, route, vn = _gmlp_layer(
                x, a_w_in[j].astype(BF16), a_ln_g[j].reshape(1, da), a_ln_b[j].reshape(1, da),
                ws2, msk2, bs2, a_w_out[j].astype(BF16), g0, b0, wr, br, alpha=alpha, n_grp=n_grp, epg=epg)
            chunk_rows.append(vn[:ns_rows].reshape(bs, ts, da))
        else:
            wqkv = b_w_qkv[j].astype(BF16)
            (q_bf,) = _proj(x, wqkv[:, :d], tables, rope=True, want_f32=False, want_mean=False,
                            n_heads=n_heads, rot=rot, name="q_proj")
            k_f, k_bf, km = _proj(x, wqkv[:, d:2 * d], tables, rope=True, want_f32=True, want_mean=True,
                                  n_heads=n_heads, rot=rot, name="k_proj")
            v_f, v_bf = _proj(x, wqkv[:, 2 * d:], None, rope=False, want_f32=True, want_mean=False,
                              n_heads=n_heads, rot=rot, name="v_proj")
            k_rows.append(k_f)
            v_rows.append(v_f)
            attn = _moba_prompt(q_bf, k_bf, v_bf, km.reshape(nt // TM, d), bp=bp, tp=tp, n_heads=n_heads, hd=hd)

            def heads(a):
                a = a[np_rows:np_rows + ns_rows].reshape(bs, ts, n_heads, hd).transpose(0, 2, 1, 3)
                return jnp.pad(a, ((0, 0), (0, 0), (0, Q_PAD - ts), (0, 0)))

            q_s, kn_s, vn_s = heads(q_bf), heads(k_bf), heads(v_bf)
            ids = _sample_gate(q_s, km_cache[:, :, j].transpose(0, 2, 1, 3))
            ids_flat = ids[:, :, :ts, :MOBA_TOPK].reshape(-1)
            o_s = _sample_attn(pt_flat, ids_flat, cache_k, cache_v, q_s, kn_s, vn_s, layer=j, ts=ts, n_pages=n_pages)
            o_s = o_s[:, :, :ts].transpose(0, 2, 1, 3).reshape(ns_rows, d).astype(BF16)
            tail = jnp.concatenate([o_s, jnp.zeros((TM - ns_rows, d), BF16)], axis=0)
            h, hg, route = _oproj_layer(attn, tail, x, b_w_o[j].astype(BF16), g0, b0, wr, br,
                                        alpha=alpha, n_grp=n_grp, epg=epg)
        plan = _moe_plan(route, n_exp)
        ys = _moe(hg.reshape(nt, s_n, LANES), plan, moe_w_gate[i].astype(BF16), moe_w_up[i].astype(BF16),
                  moe_w_down[i].astype(BF16))
        p = stream(p_prompt[i], p_sample[i])
        x = _finish(plan[4], h, ys.reshape(-1, s_n, LANES), p, ple_w_gate[i].astype(BF16),
                    ple_b_gate[i].reshape(1, d), ple_w_proj[i].astype(BF16),
                    ln_g[i, 1].reshape(1, d), ln_b[i, 1].reshape(1, d),
                    ln_g[i, 2].reshape(1, d), ln_b[i, 2].reshape(1, d), alpha=alpha)

    def prompt_rows(rows):
        return jnp.stack([a[:np_rows] for a in rows], axis=1).reshape(bp, tp, nbl, n_heads, hd)

    def sample_rows(rows):
        return jnp.stack([a[np_rows:np_rows + ns_rows] for a in rows], axis=1).reshape(bs, ts, nbl, n_heads, hd)

    y_prompt = x[:np_rows].reshape(bp, tp, d)
    y_sample = x[np_rows:np_rows + ns_rows].reshape(bs, ts, d)
    chunk_v = jnp.stack(chunk_rows, axis=2)
    return (y_prompt, y_sample, prompt_rows(k_rows), prompt_rows(v_rows),
            sample_rows(k_rows), sample_rows(v_rows), chunk_v)
```

```python
import functools

import numpy as np
import jax
import jax.numpy as jnp
from jax import lax
from jax.experimental import pallas as pl
from jax.experimental.pallas import tpu as pltpu

F32 = jnp.float32
BF16 = jnp.bfloat16

LANES = 128
TM = 256
MOBA_BLOCK = 256
MOBA_TOPK = 3
MOBA_HEADS_PER_STEP = 4
ROPE_THETA = 500000.0
LN_EPS = 1e-5
Q_PAD = 16
VMEM_LIMIT = 56 * 1024 * 1024
NEG_INF = float("-inf")
LOG2_E = 1.4426950408889634


def _cparams(n_axes):
    return pltpu.CompilerParams(dimension_semantics=("arbitrary",) * n_axes,
                                vmem_limit_bytes=VMEM_LIMIT)


def _const_spec(shape):
    zeros = (0,) * len(shape)
    return pl.BlockSpec(shape, lambda *a: zeros, pipeline_mode=pl.Buffered(1))


def _aligned(x, m):
    return x if isinstance(x, int) else pl.multiple_of(x, m)


def _ln(x, g, b):
    mu = jnp.mean(x, axis=-1, keepdims=True)
    xc = x - mu
    var = jnp.mean(xc * xc, axis=-1, keepdims=True)
    return xc * lax.rsqrt(var + LN_EPS) * g + b


def _to_slabs(ref, val):
    rows, d = val.shape
    s_n = d // LANES
    for s in range(s_n):
        ref[pl.ds(s, rows, stride=s_n), :] = val[:, s * LANES:(s + 1) * LANES]


def _from_slabs(ref, base, rows, s_n):
    cols = [ref[pl.ds(base + s, rows, stride=s_n), :] for s in range(s_n)]
    return jnp.concatenate(cols, axis=1)


def _route(h, wr_ref, br_ref, n_grp, epg):
    logits = jnp.dot(h.astype(BF16), wr_ref[...], preferred_element_type=F32) + br_ref[...]
    lane = lax.broadcasted_iota(jnp.int32, logits.shape, 1).astype(F32)
    big = float(LANES)
    lc = jnp.where(lane < n_grp, logits, NEG_INF)
    mc = jnp.max(lc, axis=1, keepdims=True)
    zc = jnp.sum(jnp.exp(lc - mc), axis=1, keepdims=True)
    pg = 1.0 / zc
    g = jnp.min(jnp.where(lc == mc, lane, big), axis=1, keepdims=True)
    lo = n_grp + g * epg
    lf = jnp.where((lane >= lo) & (lane < lo + epg), logits, NEG_INF)
    m1 = jnp.max(lf, axis=1, keepdims=True)
    i1 = jnp.min(jnp.where(lf == m1, lane, big), axis=1, keepdims=True)
    lf2 = jnp.where(lane == i1, NEG_INF, lf)
    m2 = jnp.max(lf2, axis=1, keepdims=True)
    i2 = jnp.min(jnp.where(lf2 == m2, lane, big), axis=1, keepdims=True)
    e2 = jnp.exp(m2 - m1)
    w1 = pg / (1.0 + e2)
    w2 = pg * e2 / (1.0 + e2)
    out = jnp.where(lane == 0, i1 - n_grp, 0.0)
    out = jnp.where(lane == 1, i2 - n_grp, out)
    out = jnp.where(lane == 2, w1, out)
    out = jnp.where(lane == 3, w2, out)
    return out


def _post_mix(x, mix, g_ref, b_ref, wr_ref, br_ref, h_ref, hg_ref, route_ref, alpha, n_grp, epg):
    h = _ln(alpha * x + mix, g_ref[...], b_ref[...])
    h_ref[...] = h
    _to_slabs(hg_ref, h)
    route_ref[...] = _route(h, wr_ref, br_ref, n_grp, epg)


def _gmlp_kernel(x_ref, win_ref, ag_ref, ab_ref, ws_ref, msk_ref, bs_ref, wout_ref,
                 g_ref, b_ref, wr_ref, br_ref,
                 h_ref, hg_ref, route_ref, vn_ref, gated_ref,
                 *, n_groups, chunk, alpha, n_grp, epg):
    x = x_ref[...]
    z = jnp.dot(x.astype(BF16), win_ref[...], preferred_element_type=F32)
    z = jax.nn.gelu(z)
    da = z.shape[1] // 2
    u = z[:, :da]
    vn = _ln(z[:, da:], ag_ref[...], ab_ref[...])

    @pl.when(pl.program_id(0) == pl.num_programs(0) - 1)
    def _():
        vn_ref[...] = vn

    vnb = vn.astype(BF16)
    gd = da // n_groups
    mask = msk_ref[0] > 0.0
    for g in range(n_groups):
        wsg = jnp.where(mask, ws_ref[0, g], 0.0).astype(BF16)
        for c in range(x.shape[0] // chunk):
            r0 = c * chunk
            s = jnp.dot(wsg, vnb[r0:r0 + chunk, g * gd:(g + 1) * gd], preferred_element_type=F32)
            s = s + bs_ref[0][:, g * gd:(g + 1) * gd]
            gated_ref[r0:r0 + chunk, g * gd:(g + 1) * gd] = (u[r0:r0 + chunk, g * gd:(g + 1) * gd] * s).astype(BF16)
    mix = jnp.dot(gated_ref[...], wout_ref[...], preferred_element_type=F32)
    _post_mix(x, mix, g_ref, b_ref, wr_ref, br_ref, h_ref, hg_ref, route_ref, alpha, n_grp, epg)


def _gmlp_layer(x, w_in, a_g, a_b, ws2, msk2, bs2, w_out, g0, b0, wr, br, *, alpha, n_grp, epg):
    nt, d = x.shape
    da = w_out.shape[0]
    n_groups, chunk = ws2.shape[1], ws2.shape[2]
    n_tiles = nt // TM
    last = n_tiles - 1
    s_n = d // LANES
    kern = functools.partial(_gmlp_kernel, n_groups=n_groups, chunk=chunk, alpha=alpha, n_grp=n_grp, epg=epg)
    sel = lambda i: (i // last, 0, 0, 0) if last > 0 else (0, 0, 0, 0)
    sel3 = lambda i: (i // last, 0, 0) if last > 0 else (0, 0, 0)
    return pl.pallas_call(
        kern,
        grid=(n_tiles,),
        in_specs=[
            pl.BlockSpec((TM, d), lambda i: (i, 0)),
            _const_spec((d, 2 * da)),
            _const_spec((1, da)), _const_spec((1, da)),
            pl.BlockSpec((1, n_groups, chunk, chunk), sel),
            pl.BlockSpec((1, chunk, chunk), sel3),
            pl.BlockSpec((1, chunk, da), sel3),
            _const_spec((da, d)),
            _const_spec((1, d)), _const_spec((1, d)),
            _const_spec((d, LANES)), _const_spec((1, LANES)),
        ],
        out_specs=[
            pl.BlockSpec((TM, d), lambda i: (i, 0)),
            pl.BlockSpec((TM * s_n, LANES), lambda i: (i, 0)),
            pl.BlockSpec((TM, LANES), lambda i: (i, 0)),
            pl.BlockSpec((TM, da), lambda i: (0, 0)),
        ],
        out_shape=[
            jax.ShapeDtypeStruct((nt, d), F32),
            jax.ShapeDtypeStruct((nt * s_n, LANES), F32),
            jax.ShapeDtypeStruct((nt, LANES), F32),
            jax.ShapeDtypeStruct((TM, da), F32),
        ],
        scratch_shapes=[pltpu.VMEM((TM, da), BF16)],
        compiler_params=_cparams(1),
        name="gmlp_mixer",
    )(x, w_in, a_g, a_b, ws2, msk2, bs2, w_out, g0, b0, wr, br)


def _proj_kernel(*refs, rope, outs_wanted, n_heads, rot):
    x_ref, w_ref = refs[0], refs[1]
    pos = 2
    if rope:
        cos_ref, sa_ref, sb_ref = refs[2:5]
        pos = 5
    outs = dict(zip(outs_wanted, refs[pos:]))
    y = jnp.dot(x_ref[...].astype(BF16), w_ref[...], preferred_element_type=F32)
    if rope:
        hd = y.shape[1] // n_heads
        half = rot // 2
        cos, sa, sb = cos_ref[...], sa_ref[...], sb_ref[...]
        parts = []
        for h in range(n_heads):
            yh = y[:, h * hd:(h + 1) * hd]
            parts.append(yh * cos + pltpu.roll(yh, hd - half, 1) * sa + pltpu.roll(yh, half, 1) * sb)
        y = jnp.concatenate(parts, axis=1)
    if "slabs" in outs:
        _to_slabs(outs["slabs"], y)
    if "rows" in outs:
        outs["rows"][...] = y.astype(BF16)
    if "cols" in outs:
        outs["cols"][...] = y.T.astype(BF16)
    if "mean" in outs:
        outs["mean"][0] = jnp.mean(y, axis=0, keepdims=True)


def _proj(x, w, tables, *, rope, outs_wanted, n_heads, rot, name):
    nt, d = x.shape
    dn = w.shape[1]
    n_tiles = nt // TM
    s_n = dn // LANES
    kern = functools.partial(_proj_kernel, rope=rope, outs_wanted=outs_wanted, n_heads=n_heads, rot=rot)
    in_specs = [pl.BlockSpec((TM, d), lambda i: (i, 0)), _const_spec((d, dn))]
    args = [x, w]
    if rope:
        in_specs += [pl.BlockSpec((TM, LANES), lambda i: (i, 0))] * 3
        args += list(tables)
    specs = {
        "slabs": (pl.BlockSpec((TM * s_n, LANES), lambda i: (i, 0)), jax.ShapeDtypeStruct((nt * s_n, LANES), F32)),
        "rows": (pl.BlockSpec((TM, dn), lambda i: (i, 0)), jax.ShapeDtypeStruct((nt, dn), BF16)),
        "cols": (pl.BlockSpec((dn, TM), lambda i: (0, i)), jax.ShapeDtypeStruct((dn, nt), BF16)),
        "mean": (pl.BlockSpec((1, 1, dn), lambda i: (i, 0, 0)), jax.ShapeDtypeStruct((n_tiles, 1, dn), F32)),
    }
    return pl.pallas_call(
        kern, grid=(n_tiles,), in_specs=in_specs,
        out_specs=[specs[o][0] for o in outs_wanted], out_shape=[specs[o][1] for o in outs_wanted],
        compiler_params=_cparams(1), name=name,
    )(*args)


def _top_blocks(gate, idx, topk, axis):
    big = float(gate.shape[axis])
    ids = []
    for _ in range(topk):
        m = jnp.max(gate, axis=axis, keepdims=True)
        sel = jnp.min(jnp.where(gate == m, idx, big), axis=axis, keepdims=True)
        sel = jnp.where(m > NEG_INF, sel, -1.0)
        ids.append(sel)
        gate = jnp.where(idx == sel, NEG_INF, gate)
    return ids


def _online_softmax(s, keep, m, l, c2):
    ms, ls, scales, ps = [], [], [], []
    for q0 in range(0, s.shape[1], LANES):
        qs = slice(q0, q0 + LANES)
        sh = jnp.where(keep[:, qs], s[:, qs], NEG_INF)
        m_new = jnp.maximum(m[:, qs], jnp.max(sh, axis=0, keepdims=True))
        a = jnp.exp2((m[:, qs] - m_new) * c2)
        p = jnp.exp2((sh - m_new) * c2)
        ms.append(m_new)
        ls.append(a * l[:, qs] + jnp.sum(p, axis=0, keepdims=True))
        scales.append(a)
        ps.append(p.astype(BF16))
    cat = lambda xs: jnp.concatenate(xs, axis=1)
    return cat(ms), cat(ls), cat(scales), cat(ps)


def _moba_prompt_kernel(qt_ref, k_ref, vt_ref, km_ref, o_ref, s_buf, p_buf, acc_buf, *, blk, topk, scale, hg, hd):
    n_blk = k_ref.shape[0] // blk
    kmb = km_ref[...].astype(BF16)
    key_i = lax.broadcasted_iota(jnp.int32, (blk, blk), 0)
    qry_i = lax.broadcasted_iota(jnp.int32, (blk, blk), 1)
    causal = key_i <= qry_i
    blk_i = lax.broadcasted_iota(jnp.int32, (n_blk, blk), 0).astype(F32)
    c2 = scale * LOG2_E

    def q_tile(qi, carry):
        q0 = _aligned(qi * blk, blk)
        qf = jnp.asarray(qi, F32)
        ids, init = [], []
        for h in range(hg):
            hs = slice(h * hd, (h + 1) * hd)
            qt = qt_ref[hs, pl.ds(q0, blk)]
            s_buf[h] = jnp.dot(k_ref[pl.ds(0, blk), hs], qt, preferred_element_type=F32)
            gate = jnp.dot(kmb[:, hs], qt, preferred_element_type=F32)
            gate = jnp.where(blk_i < qf, gate, NEG_INF)
            ids.append(_top_blocks(gate, blk_i, topk, 0))
            s = jnp.dot(k_ref[pl.ds(q0, blk), hs], qt, preferred_element_type=F32)
            m, l, _, p = _online_softmax(s, causal, jnp.full((1, blk), NEG_INF, F32), jnp.zeros((1, blk), F32), c2)
            p_buf[h] = p
            acc_buf[h] = jnp.zeros((hd, blk), F32)
            init.append((m, l, jnp.zeros((1, blk), F32)))

        def kv_step(n, c):
            prev0, stats = c
            prev0 = _aligned(prev0, blk)
            nf = jnp.asarray(n, F32)
            nxt0 = _aligned(jnp.minimum(n + 1, qi - 1) * blk, blk)
            out = []
            for h in range(hg):
                hs = slice(h * hd, (h + 1) * hd)
                m, l, a_prev = stats[h]
                s = s_buf[h]
                s_buf[h] = jnp.dot(k_ref[pl.ds(nxt0, blk), hs], qt_ref[hs, pl.ds(q0, blk)], preferred_element_type=F32)
                pv = jnp.dot(vt_ref[hs, pl.ds(prev0, blk)], p_buf[h], preferred_element_type=F32)
                acc_buf[h] = a_prev * acc_buf[h] + pv
                picked = ids[h][0] == nf
                for t in range(1, topk):
                    picked = picked | (ids[h][t] == nf)
                m, l, a, p = _online_softmax(s, picked, m, l, c2)
                p_buf[h] = p
                out.append((m, l, a))
            return _aligned(n * blk, blk), tuple(out)

        prev0, fin = lax.fori_loop(0, qi, kv_step, (jnp.asarray(q0, jnp.int32), tuple(init)))
        prev0 = _aligned(prev0, blk)
        for h in range(hg):
            hs = slice(h * hd, (h + 1) * hd)
            m, l, a_prev = fin[h]
            acc = a_prev * acc_buf[h] + jnp.dot(vt_ref[hs, pl.ds(prev0, blk)], p_buf[h], preferred_element_type=F32)
            o_ref[pl.ds(q0, blk), hs] = (acc / l).T.astype(o_ref.dtype)
        return carry

    lax.fori_loop(0, n_blk, q_tile, 0)


def _moba_prompt(qt, k, vt, km2, *, bp, tp, n_heads, hd):
    d = k.shape[1]
    hg = min(MOBA_HEADS_PER_STEP, n_heads)
    n_blk = tp // MOBA_BLOCK
    kern = functools.partial(_moba_prompt_kernel, blk=MOBA_BLOCK, topk=MOBA_TOPK, scale=hd ** -0.5, hg=hg, hd=hd)
    row_spec = pl.BlockSpec((tp, hg * hd), lambda b, g: (b, g))
    col_spec = pl.BlockSpec((hg * hd, tp), lambda b, g: (g, b))
    return pl.pallas_call(
        kern, grid=(bp, n_heads // hg),
        in_specs=[col_spec, row_spec, col_spec, pl.BlockSpec((n_blk, hg * hd), lambda b, g: (b, g))],
        out_specs=row_spec,
        out_shape=jax.ShapeDtypeStruct((bp * tp, d), BF16),
        scratch_shapes=[pltpu.VMEM((hg, MOBA_BLOCK, MOBA_BLOCK), F32), pltpu.VMEM((hg, MOBA_BLOCK, MOBA_BLOCK), BF16),
                        pltpu.VMEM((hg, hd, MOBA_BLOCK), F32)],
        compiler_params=_cparams(2), name="moba_prompt",
    )(qt, k, vt, km2)


def _cache_mean_kernel(pt_ref, *refs, ppb, inv):
    o_ref = refs[ppb]
    acc = jnp.sum(refs[0][...], axis=0)
    for t in range(1, ppb):
        acc = acc + jnp.sum(refs[t][...], axis=0)
    o_ref[0, 0] = acc * inv


def _cache_means(cache_k, pt_flat, *, bs, n_pages):
    _, page, nbl, n_heads, hd = cache_k.shape
    ppb = MOBA_BLOCK // page
    n_cblk = n_pages // ppb
    kern = functools.partial(_cache_mean_kernel, ppb=ppb, inv=1.0 / MOBA_BLOCK)

    def page_spec(t):
        return pl.BlockSpec((None, page, nbl, n_heads, hd),
                            lambda b, n, pt: (pt[b * n_pages + n * ppb + t], 0, 0, 0, 0))

    return pl.pallas_call(
        kern,
        grid_spec=pltpu.PrefetchScalarGridSpec(
            num_scalar_prefetch=1, grid=(bs, n_cblk),
            in_specs=[page_spec(t) for t in range(ppb)],
            out_specs=pl.BlockSpec((1, 1, nbl, n_heads, hd), lambda b, n, pt: (b, n, 0, 0, 0)),
        ),
        out_shape=jax.ShapeDtypeStruct((bs, n_cblk, nbl, n_heads, hd), F32),
        compiler_params=_cparams(2), name="cache_block_means",
    )(pt_flat, *([cache_k] * ppb))


def _sample_gate_kernel(q_ref, km_ref, o_ref, *, topk):
    n_heads = q_ref.shape[1]
    n_cblk = km_ref.shape[2]
    nt_dims = (((1,), (1,)), ((), ()))
    lane = lax.broadcasted_iota(jnp.int32, (Q_PAD, n_cblk), 1).astype(F32)
    out_lane = lax.broadcasted_iota(jnp.int32, (Q_PAD, LANES), 1)
    for h in range(n_heads):
        gate = lax.dot_general(q_ref[0, h], km_ref[0, h].astype(BF16), nt_dims, preferred_element_type=F32)
        ids = _top_blocks(gate, lane, topk, 1)
        out = jnp.zeros((Q_PAD, LANES), F32)
        for t in range(topk):
            out = jnp.where(out_lane == t, ids[t], out)
        o_ref[0, h] = out.astype(jnp.int32)


def _sample_gate(q_s, km_s):
    bs, n_heads, _, hd = q_s.shape
    n_cblk = km_s.shape[2]
    kern = functools.partial(_sample_gate_kernel, topk=MOBA_TOPK)
    return pl.pallas_call(
        kern, grid=(bs,),
        in_specs=[pl.BlockSpec((1, n_heads, Q_PAD, hd), lambda b: (b, 0, 0, 0)),
                  pl.BlockSpec((1, n_heads, n_cblk, hd), lambda b: (b, 0, 0, 0))],
        out_specs=pl.BlockSpec((1, n_heads, Q_PAD, LANES), lambda b: (b, 0, 0, 0)),
        out_shape=jax.ShapeDtypeStruct((bs, n_heads, Q_PAD, LANES), jnp.int32),
        compiler_params=_cparams(1), name="sample_gate",
    )(q_s, km_s)


def _sample_attn_kernel(pt_ref, ids_ref, ck_hbm, cv_hbm, q_ref, kn_ref, vn_ref, o_ref, kbuf, vbuf, sem,
                        *, layer, ts, topk, ppb, n_pages, scale):
    b, h = pl.program_id(0), pl.program_id(1)
    n_heads = pl.num_programs(1)
    step = b * n_heads + h
    n_steps = pl.num_programs(0) * n_heads
    page = kbuf.shape[2]

    def copies(bb, hh, slot):
        out = []
        for qi in range(ts):
            for t in range(topk):
                blk = ids_ref[((bb * n_heads + hh) * ts + qi) * topk + t]
                for half in range(ppb):
                    pg = pt_ref[bb * n_pages + blk * ppb + half]
                    j = (qi * topk + t) * ppb + half
                    out.append(pltpu.make_async_copy(ck_hbm.at[pg, :, layer, hh, :], kbuf.at[slot, j], sem.at[0, slot]))
                    out.append(pltpu.make_async_copy(cv_hbm.at[pg, :, layer, hh, :], vbuf.at[slot, j], sem.at[1, slot]))
        return out

    @pl.when(step == 0)
    def _():
        for c in copies(b, h, 0):
            c.start()

    @pl.when(step + 1 < n_steps)
    def _():
        nxt = step + 1
        for c in copies(nxt // n_heads, nxt % n_heads, nxt % 2):
            c.start()

    slot = step % 2
    for c in copies(b, h, slot):
        c.wait()

    nt_dims = (((1,), (1,)), ((), ()))
    q = q_ref[0, 0]
    rows = lax.broadcasted_iota(jnp.int32, (Q_PAD, Q_PAD), 0)
    cols = lax.broadcasted_iota(jnp.int32, (Q_PAD, Q_PAD), 1)
    s_new = lax.dot_general(q, kn_ref[0, 0], nt_dims, preferred_element_type=F32) * scale
    s_new = jnp.where((cols <= rows) & (cols < ts), s_new, NEG_INF)
    out_rows = lax.broadcasted_iota(jnp.int32, (Q_PAD, q.shape[1]), 0)
    out = jnp.zeros((Q_PAD, q.shape[1]), F32)
    per_q = topk * ppb
    for qi in range(ts):
        kq = kbuf[slot, qi * per_q:(qi + 1) * per_q].reshape(per_q * page, -1).astype(BF16)
        vq = vbuf[slot, qi * per_q:(qi + 1) * per_q].reshape(per_q * page, -1).astype(BF16)
        s = lax.dot_general(q, kq, nt_dims, preferred_element_type=F32) * scale
        m = jnp.maximum(jnp.max(s, axis=1, keepdims=True), jnp.max(s_new, axis=1, keepdims=True))
        p = jnp.exp(s - m)
        pn = jnp.exp(s_new - m)
        l = jnp.sum(p, axis=1, keepdims=True) + jnp.sum(pn, axis=1, keepdims=True)
        o = jnp.dot(p.astype(BF16), vq, preferred_element_type=F32)
        o = o + jnp.dot(pn.astype(BF16), vn_ref[0, 0], preferred_element_type=F32)
        out = jnp.where(out_rows == qi, o / l, out)
    o_ref[0, 0] = out


def _sample_attn(pt_flat, ids_flat, cache_k, cache_v, q_s, kn_s, vn_s, *, layer, ts, n_pages):
    bs, n_heads, _, hd = q_s.shape
    page = cache_k.shape[1]
    ppb = MOBA_BLOCK // page
    n_sel = ts * MOBA_TOPK * ppb
    kern = functools.partial(_sample_attn_kernel, layer=layer, ts=ts, topk=MOBA_TOPK, ppb=ppb,
                             n_pages=n_pages, scale=hd ** -0.5)
    qspec = pl.BlockSpec((1, 1, Q_PAD, hd), lambda b, h, pt, ids: (b, h, 0, 0))
    return pl.pallas_call(
        kern,
        grid_spec=pltpu.PrefetchScalarGridSpec(
            num_scalar_prefetch=2, grid=(bs, n_heads),
            in_specs=[pl.BlockSpec(memory_space=pl.ANY), pl.BlockSpec(memory_space=pl.ANY), qspec, qspec, qspec],
            out_specs=qspec,
            scratch_shapes=[pltpu.VMEM((2, n_sel, page, hd), F32), pltpu.VMEM((2, n_sel, page, hd), F32),
                            pltpu.SemaphoreType.DMA((2, 2))],
        ),
        out_shape=jax.ShapeDtypeStruct((bs, n_heads, Q_PAD, hd), F32),
        compiler_params=_cparams(2), name="sample_attn",
    )(pt_flat, ids_flat, cache_k, cache_v, q_s, kn_s, vn_s)


def _oproj_kernel(a_ref, tail_ref, x_ref, wo_ref, g_ref, b_ref, wr_ref, br_ref, h_ref, hg_ref, route_ref,
                  *, alpha, n_grp, epg):
    is_tail = pl.program_id(0) == pl.num_programs(0) - 1
    a = jnp.where(is_tail, tail_ref[...], a_ref[...])
    mix = jnp.dot(a, wo_ref[...], preferred_element_type=F32)
    _post_mix(x_ref[...], mix, g_ref, b_ref, wr_ref, br_ref, h_ref, hg_ref, route_ref, alpha, n_grp, epg)


def _oproj_layer(attn, tail, x, w_o, g0, b0, wr, br, *, alpha, n_grp, epg):
    nt, d = x.shape
    s_n = d // LANES
    last_prompt = attn.shape[0] // TM - 1
    kern = functools.partial(_oproj_kernel, alpha=alpha, n_grp=n_grp, epg=epg)
    return pl.pallas_call(
        kern, grid=(nt // TM,),
        in_specs=[pl.BlockSpec((TM, d), lambda i: (jnp.minimum(i, last_prompt), 0)),
                  pl.BlockSpec((TM, d), lambda i: (0, 0)),
                  pl.BlockSpec((TM, d), lambda i: (i, 0)),
                  _const_spec((d, d)), _const_spec((1, d)), _const_spec((1, d)),
                  _const_spec((d, LANES)), _const_spec((1, LANES))],
        out_specs=[pl.BlockSpec((TM, d), lambda i: (i, 0)),
                   pl.BlockSpec((TM * s_n, LANES), lambda i: (i, 0)),
                   pl.BlockSpec((TM, LANES), lambda i: (i, 0))],
        out_shape=[jax.ShapeDtypeStruct((nt, d), F32),
                   jax.ShapeDtypeStruct((nt * s_n, LANES), F32),
                   jax.ShapeDtypeStruct((nt, LANES), F32)],
        compiler_params=_cparams(1), name="attn_out_proj",
    )(attn, tail, x, w_o, g0, b0, wr, br)


GATHER_UNROLL = 16


def _start_row_gather(idx_ref, idx_base, n_rows, src_hbm, dst, dst_base, s_n, sem):
    def body(rr, c):
        for k in range(GATHER_UNROLL):
            r = rr * GATHER_UNROLL + k
            row = idx_ref[idx_base + r]
            d0 = _aligned((dst_base + r) * s_n, s_n)
            pltpu.make_async_copy(src_hbm.at[row], dst.at[pl.ds(d0, s_n)], sem).start()
        return c
    lax.fori_loop(0, n_rows // GATHER_UNROLL, body, 0)


def _start_row_scatter(idx_ref, idx_base, n_rows, src_hbm, src_base, dst_hbm, sem):
    def body(rr, c):
        for k in range(GATHER_UNROLL):
            r = rr * GATHER_UNROLL + k
            pltpu.make_async_copy(src_hbm.at[src_base + r], dst_hbm.at[idx_ref[idx_base + r]], sem).start()
        return c
    lax.fori_loop(0, n_rows // GATHER_UNROLL, body, 0)


def _moe_plan(route, n_exp):
    nt = route.shape[0]
    e_flat = route[:, :2].astype(jnp.int32).T.reshape(-1)
    ar = jnp.arange(n_exp, dtype=jnp.int32)
    onehot = (e_flat[:, None] == ar[None, :]).astype(jnp.int32)
    csum = jnp.cumsum(onehot, axis=0)
    counts = csum[-1]
    padded = ((counts + TM - 1) // TM) * TM
    pend = jnp.cumsum(padded)
    pstart = pend - padded
    pos = jnp.sum(onehot * (csum - 1 + pstart[None, :]), axis=1)
    n_tiles = (2 * nt + n_exp * TM) // TM
    tile_start = jnp.arange(n_tiles, dtype=jnp.int32) * TM
    tile_e = jnp.sum((tile_start[:, None] >= pend[None, :]).astype(jnp.int32), axis=1)
    tile_e = jnp.minimum(tile_e, jnp.max(jnp.where(counts > 0, ar, 0)))
    n_used = (pend[-1] // TM).reshape(1)
    i32 = lambda a: a.astype(jnp.int32)
    return i32(pos), i32(tile_e), i32(n_used), i32(pend), i32(padded - counts)


def _dispatch_kernel(pos_ref, pend_ref, npad_ref, hg_hbm, xs_hbm, zbuf, sem, zsem, *, nt, n_exp):
    i = pl.program_id(0)
    n_rows = xs_hbm.shape[0]

    def zero_rows(first, count):
        def body(r, c):
            pltpu.make_async_copy(zbuf, xs_hbm.at[first + r], zsem).start()
            return c
        lax.fori_loop(0, count, body, 0)

    @pl.when(i == 0)
    def _():
        zbuf[...] = jnp.zeros_like(zbuf)
        for e in range(n_exp):
            zero_rows(pend_ref[e] - npad_ref[e], npad_ref[e])
        zero_rows(pend_ref[n_exp - 1], n_rows - pend_ref[n_exp - 1])

    for k in range(2):
        _start_row_scatter(pos_ref, k * nt + i * TM, TM, hg_hbm, i * TM, xs_hbm, sem)
    for k in range(2):
        pltpu.make_async_copy(hg_hbm.at[pl.ds(0, TM)], xs_hbm.at[pl.ds(0, TM)], sem).wait()

    @pl.when(i == pl.num_programs(0) - 1)
    def _():
        n_zero = n_exp * TM
        pltpu.make_async_copy(xs_hbm.at[pl.ds(0, n_zero)], xs_hbm.at[pl.ds(0, n_zero)], zsem).wait()


def _dispatch(hg3, pos, pend, npad, n_rows):
    nt, s_n, _ = hg3.shape
    n_exp = pend.shape[0]
    kern = functools.partial(_dispatch_kernel, nt=nt, n_exp=n_exp)
    return pl.pallas_call(
        kern,
        grid_spec=pltpu.PrefetchScalarGridSpec(
            num_scalar_prefetch=3, grid=(nt // TM,),
            in_specs=[pl.BlockSpec(memory_space=pl.ANY)],
            out_specs=pl.BlockSpec(memory_space=pl.ANY),
            scratch_shapes=[pltpu.VMEM((s_n, LANES), F32), pltpu.SemaphoreType.DMA(()), pltpu.SemaphoreType.DMA(())],
        ),
        out_shape=jax.ShapeDtypeStruct((n_rows, s_n, LANES), F32),
        compiler_params=_cparams(1), name="moe_dispatch",
    )(pos, pend, npad, hg3)


def _moe_kernel(te_ref, nu_ref, x_ref, wg_ref, wu_ref, wd_ref, y_ref, wgb, wub, wdb):
    i = pl.program_id(0)
    s_n = wg_ref.shape[1] // LANES
    rows = x_ref.shape[0] // s_n

    @pl.when(i < nu_ref[0])
    def _():
        @pl.when((i == 0) | (te_ref[i] != te_ref[jnp.maximum(i - 1, 0)]))
        def _():
            wgb[...] = wg_ref[0].astype(BF16)
            wub[...] = wu_ref[0].astype(BF16)
            wdb[...] = wd_ref[0].astype(BF16)

        x = _from_slabs(x_ref, 0, rows, s_n).astype(BF16)
        a = jnp.dot(x, wgb[...], preferred_element_type=F32)
        u = jnp.dot(x, wub[...], preferred_element_type=F32)
        hid = (a * jax.nn.sigmoid(a)) * u
        _to_slabs(y_ref, jnp.dot(hid.astype(BF16), wdb[...], preferred_element_type=F32))

    @pl.when(i >= nu_ref[0])
    def _():
        y_ref[...] = jnp.zeros_like(y_ref)


def _moe(xs2, tile_e, n_used, w_gate, w_up, w_down):
    n_tiles = tile_e.shape[0]
    _, d, f = w_gate.shape
    s_n = d // LANES
    tile = lambda i, te, nu: (jnp.minimum(i, nu[0] - 1), 0)
    wspec = lambda shape: pl.BlockSpec((1,) + shape, lambda i, te, nu: (te[i], 0, 0))
    return pl.pallas_call(
        _moe_kernel,
        grid_spec=pltpu.PrefetchScalarGridSpec(
            num_scalar_prefetch=2, grid=(n_tiles,),
            in_specs=[pl.BlockSpec((TM * s_n, LANES), tile), wspec((d, f)), wspec((d, f)), wspec((f, d))],
            out_specs=pl.BlockSpec((TM * s_n, LANES), lambda i, te, nu: (i, 0)),
            scratch_shapes=[pltpu.VMEM((d, f), BF16), pltpu.VMEM((d, f), BF16), pltpu.VMEM((f, d), BF16)],
        ),
        out_shape=jax.ShapeDtypeStruct((n_tiles * TM * s_n, LANES), F32),
        compiler_params=_cparams(1), name="moe_experts",
    )(tile_e, n_used, xs2, w_gate, w_up, w_down)


def _finish_kernel(pos_ref, h_ref, r_ref, ys_hbm, p_ref, wpg_ref, bpg_ref, wpp_ref, g1_ref, b1_ref, g2_ref, b2_ref,
                   o_ref, ybuf, sem, *, alpha, nt):
    i = pl.program_id(0)
    n = pl.num_programs(0)
    s_n = ys_hbm.shape[1]
    rows = h_ref.shape[0]

    def gather(tile, slot):
        for k in range(2):
            _start_row_gather(pos_ref, k * nt + tile * rows, rows, ys_hbm, ybuf.at[slot], k * rows, s_n, sem.at[slot])

    @pl.when(i == 0)
    def _():
        gather(0, 0)

    @pl.when(i + 1 < n)
    def _():
        gather(i + 1, (i + 1) % 2)

    slot = i % 2
    pltpu.make_async_copy(ybuf.at[slot], ybuf.at[slot], sem.at[slot]).wait()
    y2 = ybuf.at[slot]
    r = r_ref[...]
    moe = r[:, 2:3] * _from_slabs(y2, 0, rows, s_n) + r[:, 3:4] * _from_slabs(y2, rows * s_n, rows, s_n)
    h2 = _ln(alpha * h_ref[...] + moe, g1_ref[...], b1_ref[...])
    zg = jnp.dot(h2.astype(BF16), wpg_ref[...], preferred_element_type=F32) + bpg_ref[...]
    zp = jnp.dot(p_ref[...].astype(BF16), wpp_ref[...], preferred_element_type=F32)
    z = jax.nn.sigmoid(zg) * zp
    o_ref[...] = h2 + _ln(z, g2_ref[...], b2_ref[...])


def _finish(pos, h, route, ys3, p, w_pg, b_pg, w_pp, g1, b1, g2, b2, *, alpha):
    nt, d = h.shape
    dp = p.shape[1]
    s_n = d // LANES
    kern = functools.partial(_finish_kernel, alpha=alpha, nt=nt)
    tile = lambda i, pos: (i, 0)
    const = lambda shape: pl.BlockSpec(shape, lambda i, pos: (0,) * len(shape), pipeline_mode=pl.Buffered(1))
    return pl.pallas_call(
        kern,
        grid_spec=pltpu.PrefetchScalarGridSpec(
            num_scalar_prefetch=1, grid=(nt // TM,),
            in_specs=[pl.BlockSpec((TM, d), tile), pl.BlockSpec((TM, LANES), tile), pl.BlockSpec(memory_space=pl.ANY),
                      pl.BlockSpec((TM, dp), tile), const((d, d)), const((1, d)), const((dp, d)),
                      const((1, d)), const((1, d)), const((1, d)), const((1, d))],
            out_specs=pl.BlockSpec((TM, d), tile),
            scratch_shapes=[pltpu.VMEM((2, 2 * TM * s_n, LANES), F32), pltpu.SemaphoreType.DMA((2,))],
        ),
        out_shape=jax.ShapeDtypeStruct((nt, d), F32),
        compiler_params=_cparams(1), name="layer_finish",
    )(pos, h, route, ys3, p, w_pg, b_pg, w_pp, g1, b1, g2, b2)


def _rope_tables(positions, hd, rot):
    half = rot // 2
    inv = ROPE_THETA ** (-jnp.arange(half, dtype=F32) / half)
    ang = positions.astype(F32)[:, None] * inv[None, :]
    cos, sin = jnp.cos(ang), jnp.sin(ang)
    n = positions.shape[0]
    ones = jnp.ones((n, hd - rot), F32)
    zeros_h = jnp.zeros((n, half), F32)
    zeros_r = jnp.zeros((n, hd - rot), F32)
    cos_t = jnp.concatenate([cos, cos, ones], axis=1)
    sa = jnp.concatenate([-sin, zeros_h, zeros_r], axis=1)
    sb = jnp.concatenate([zeros_h, sin, zeros_r], axis=1)
    return cos_t, sa, sb


def kernel(x_prompt, x_sample, cache_k, cache_v, page_table, p_prompt, p_sample, a_w_in, a_ln_g, a_ln_b, a_w_s, a_b_s, a_w_out, b_w_qkv, b_w_o, ln_g, ln_b, moe_w_rc, moe_b_rc, moe_w_rf, moe_b_rf, moe_w_gate, moe_w_up, moe_w_down, ple_w_proj, ple_w_gate, ple_b_gate):
    bp, tp, d = x_prompt.shape
    bs, ts, _ = x_sample.shape
    depth = ln_g.shape[0]
    _, page, nbl, n_heads, hd = cache_k.shape
    n_pages = page_table.shape[1]
    past_len = n_pages * page
    n_groups, chunk = a_w_s.shape[1], a_w_s.shape[2]
    da = a_w_out.shape[1]
    n_grp, epg = moe_w_rf.shape[2], moe_w_rf.shape[3]
    n_exp = n_grp * epg
    rot = hd // 4
    alpha = (2.0 * depth) ** 0.25
    np_rows, ns_rows = bp * tp, bs * ts
    nt = np_rows + TM
    s_n = d // LANES

    assert tp % MOBA_BLOCK == 0 and past_len % MOBA_BLOCK == 0 and MOBA_BLOCK % page == 0
    assert ts <= min(chunk, Q_PAD) and chunk % ts == 0 and ns_rows <= chunk and TM % chunk == 0
    assert n_heads * hd == d and hd == LANES and n_grp + n_exp <= LANES and n_pages // (MOBA_BLOCK // page) >= MOBA_TOPK

    def stream(prompt, sample):
        width = prompt.shape[-1]
        pad = jnp.zeros((TM - ns_rows, width), prompt.dtype)
        return jnp.concatenate([prompt.reshape(np_rows, width), sample.reshape(ns_rows, width), pad], axis=0)

    x = stream(x_prompt, x_sample)
    pt_flat = page_table.reshape(-1).astype(jnp.int32)

    positions = jnp.concatenate([jnp.tile(jnp.arange(tp, dtype=jnp.int32), bp),
                                 jnp.tile(past_len + jnp.arange(ts, dtype=jnp.int32), bs),
                                 jnp.zeros((TM - ns_rows,), jnp.int32)])
    tables = _rope_tables(positions, hd, rot)

    r = np.arange(chunk)
    tril = (r[None, :] <= r[:, None])
    same_seq = (r[None, :] // ts == r[:, None] // ts) & (r[:, None] < ns_rows) & (r[None, :] < ns_rows)
    msk2 = jnp.asarray(np.stack([tril, tril & same_seq]).astype(np.float32))

    km_cache = _cache_means(cache_k, pt_flat, bs=bs, n_pages=n_pages)
    n_sorted = 2 * nt + n_exp * TM

    k_rows, v_rows, chunk_rows = [], [], []
    for i in range(depth):
        j = i // 2
        wr = jnp.concatenate([moe_w_rc[i], moe_w_rf[i].reshape(d, n_exp),
                              jnp.zeros((d, LANES - n_grp - n_exp), F32)], axis=1).astype(BF16)
        br = jnp.concatenate([moe_b_rc[i], moe_b_rf[i].reshape(n_exp),
                              jnp.zeros((LANES - n_grp - n_exp,), F32)]).reshape(1, LANES)
        g0, b0 = ln_g[i, 0].reshape(1, d), ln_b[i, 0].reshape(1, d)
        if i % 2 == 0:
            ws_s = jnp.tile(a_w_s[j][:, :ts, :ts], (1, chunk // ts, chunk // ts))
            ws2 = jnp.stack([a_w_s[j], ws_s])
            bs_p = jnp.repeat(a_b_s[j].T, da // n_groups, axis=1)
            bs_s = jnp.repeat(jnp.tile(a_b_s[j][:, :ts], (1, chunk // ts)).T, da // n_groups, axis=1)
            bs2 = jnp.stack([bs_p, bs_s])
            h, hg, route, vn = _gmlp_layer(
                x, a_w_in[j].astype(BF16), a_ln_g[j].reshape(1, da), a_ln_b[j].reshape(1, da),
                ws2, msk2, bs2, a_w_out[j].astype(BF16), g0, b0, wr, br, alpha=alpha, n_grp=n_grp, epg=epg)
            chunk_rows.append(vn[:ns_rows].reshape(bs, ts, da))
        else:
            wqkv = b_w_qkv[j].astype(BF16)
            (q_t,) = _proj(x, wqkv[:, :d], tables, rope=True, outs_wanted=("cols",),
                           n_heads=n_heads, rot=rot, name="q_proj")
            k_sl, k_bf, km = _proj(x, wqkv[:, d:2 * d], tables, rope=True, outs_wanted=("slabs", "rows", "mean"),
                                   n_heads=n_heads, rot=rot, name="k_proj")
            v_sl, v_t = _proj(x, wqkv[:, 2 * d:], None, rope=False, outs_wanted=("slabs", "cols"),
                              n_heads=n_heads, rot=rot, name="v_proj")
            k_sl, v_sl = k_sl.reshape(nt, n_heads, hd), v_sl.reshape(nt, n_heads, hd)
            k_rows.append(k_sl)
            v_rows.append(v_sl)
            attn = _moba_prompt(q_t, k_bf, v_t, km.reshape(nt // TM, d), bp=bp, tp=tp, n_heads=n_heads, hd=hd)

            def heads(a):
                a = a.reshape(bs, ts, n_heads, hd).transpose(0, 2, 1, 3).astype(BF16)
                return jnp.pad(a, ((0, 0), (0, 0), (0, Q_PAD - ts), (0, 0)))

            sl = slice(np_rows, np_rows + ns_rows)
            q_s, kn_s, vn_s = heads(q_t[:, sl].T), heads(k_bf[sl]), heads(v_sl[sl])
            ids = _sample_gate(q_s, km_cache[:, :, j].transpose(0, 2, 1, 3))
            ids_flat = ids[:, :, :ts, :MOBA_TOPK].reshape(-1)
            o_s = _sample_attn(pt_flat, ids_flat, cache_k, cache_v, q_s, kn_s, vn_s, layer=j, ts=ts, n_pages=n_pages)
            o_s = o_s[:, :, :ts].transpose(0, 2, 1, 3).reshape(ns_rows, d).astype(BF16)
            tail = jnp.concatenate([o_s, jnp.zeros((TM - ns_rows, d), BF16)], axis=0)
            h, hg, route = _oproj_layer(attn, tail, x, b_w_o[j].astype(BF16), g0, b0, wr, br,
                                        alpha=alpha, n_grp=n_grp, epg=epg)
        pos, tile_e, n_used, pend, npad = _moe_plan(route, n_exp)
        xs = _dispatch(hg.reshape(nt, s_n, LANES), pos, pend, npad, n_sorted)
        ys = _moe(xs.reshape(n_sorted * s_n, LANES), tile_e, n_used, moe_w_gate[i], moe_w_up[i], moe_w_down[i])
        p = stream(p_prompt[i], p_sample[i])
        x = _finish(pos, h, route, ys.reshape(n_sorted, s_n, LANES), p, ple_w_gate[i].astype(BF16),
                    ple_b_gate[i].reshape(1, d), ple_w_proj[i].astype(BF16),
                    ln_g[i, 1].reshape(1, d), ln_b[i, 1].reshape(1, d),
                    ln_g[i, 2].reshape(1, d), ln_b[i, 2].reshape(1, d), alpha=alpha)

    def prompt_rows(rows):
        return jnp.stack([a[:np_rows] for a in rows], axis=1).reshape(bp, tp, nbl, n_heads, hd)

    def sample_rows(rows):
        return jnp.stack([a[np_rows:np_rows + ns_rows] for a in rows], axis=1).reshape(bs, ts, nbl, n_heads, hd)

    y_prompt = x[:np_rows].reshape(bp, tp, d)
    y_sample = x[np_rows:np_rows + ns_rows].reshape(bs, ts, d)
    chunk_v = jnp.stack(chunk_rows, axis=2)
    return (y_prompt, y_sample, prompt_rows(k_rows), prompt_rows(v_rows),
            sample_rows(k_rows), sample_rows(v_rows), chunk_v)
```

```python
import functools

import numpy as np
import jax
import jax.numpy as jnp
from jax import lax
from jax.experimental import pallas as pl
from jax.experimental.pallas import tpu as pltpu

F32 = jnp.float32
BF16 = jnp.bfloat16

LANES = 128
TM = 256
MOBA_BLOCK = 256
MOBA_TOPK = 3
MOBA_HEADS_PER_STEP = 4
ROPE_THETA = 500000.0
LN_EPS = 1e-5
Q_PAD = 16
VMEM_LIMIT = 56 * 1024 * 1024
NEG_INF = float("-inf")
LOG2_E = 1.4426950408889634


def _cparams(n_axes):
    return pltpu.CompilerParams(dimension_semantics=("arbitrary",) * n_axes,
                                vmem_limit_bytes=VMEM_LIMIT)


def _const_spec(shape):
    zeros = (0,) * len(shape)
    return pl.BlockSpec(shape, lambda *a: zeros, pipeline_mode=pl.Buffered(1))


def _layer_spec(shape, idx):
    return pl.BlockSpec((None,) + shape, lambda *a: idx, pipeline_mode=pl.Buffered(1))


def _aligned(x, m):
    return x if isinstance(x, int) else pl.multiple_of(x, m)


def _ln(x, g, b):
    mu = jnp.mean(x, axis=-1, keepdims=True)
    xc = x - mu
    var = jnp.mean(xc * xc, axis=-1, keepdims=True)
    return xc * lax.rsqrt(var + LN_EPS) * g + b


def _to_slabs(ref, val):
    rows, d = val.shape
    s_n = d // LANES
    for s in range(s_n):
        ref[pl.ds(s, rows, stride=s_n), :] = val[:, s * LANES:(s + 1) * LANES]


def _from_slabs(ref, base, rows, s_n):
    cols = [ref[pl.ds(base + s, rows, stride=s_n), :] for s in range(s_n)]
    return jnp.concatenate(cols, axis=1)


def _route(h, wr_ref, br_ref, n_grp, epg):
    logits = jnp.dot(h.astype(BF16), wr_ref[...], preferred_element_type=F32) + br_ref[...]
    lane = lax.broadcasted_iota(jnp.int32, logits.shape, 1).astype(F32)
    big = float(LANES)
    lc = jnp.where(lane < n_grp, logits, NEG_INF)
    mc = jnp.max(lc, axis=1, keepdims=True)
    zc = jnp.sum(jnp.exp(lc - mc), axis=1, keepdims=True)
    pg = 1.0 / zc
    g = jnp.min(jnp.where(lc == mc, lane, big), axis=1, keepdims=True)
    lo = n_grp + g * epg
    lf = jnp.where((lane >= lo) & (lane < lo + epg), logits, NEG_INF)
    m1 = jnp.max(lf, axis=1, keepdims=True)
    i1 = jnp.min(jnp.where(lf == m1, lane, big), axis=1, keepdims=True)
    lf2 = jnp.where(lane == i1, NEG_INF, lf)
    m2 = jnp.max(lf2, axis=1, keepdims=True)
    i2 = jnp.min(jnp.where(lf2 == m2, lane, big), axis=1, keepdims=True)
    e2 = jnp.exp(m2 - m1)
    w1 = pg / (1.0 + e2)
    w2 = pg * e2 / (1.0 + e2)
    out = jnp.where(lane == 0, i1 - n_grp, 0.0)
    out = jnp.where(lane == 1, i2 - n_grp, out)
    out = jnp.where(lane == 2, w1, out)
    out = jnp.where(lane == 3, w2, out)
    return out


def _post_mix(x, mix, g_ref, b_ref, wr_ref, br_ref, h_ref, hg_ref, route_ref, alpha, n_grp, epg):
    h = _ln(alpha * x + mix, g_ref[...], b_ref[...])
    h_ref[...] = h
    _to_slabs(hg_ref, h)
    route_ref[...] = _route(h, wr_ref, br_ref, n_grp, epg)


def _gmlp_kernel(x_ref, win_ref, ag_ref, ab_ref, ws_ref, msk_ref, bs_ref, wout_ref,
                 g_ref, b_ref, wr_ref, br_ref,
                 h_ref, hg_ref, route_ref, vn_ref, gated_ref,
                 *, n_groups, chunk, alpha, n_grp, epg):
    x = x_ref[...]
    z = jnp.dot(x.astype(BF16), win_ref[...], preferred_element_type=F32)
    z = jax.nn.gelu(z)
    da = z.shape[1] // 2
    u = z[:, :da]
    vn = _ln(z[:, da:], ag_ref[...], ab_ref[...])

    @pl.when(pl.program_id(0) == pl.num_programs(0) - 1)
    def _():
        vn_ref[...] = vn

    vnb = vn.astype(BF16)
    gd = da // n_groups
    mask = msk_ref[0] > 0.0
    for g in range(n_groups):
        wsg = jnp.where(mask, ws_ref[0, g], 0.0).astype(BF16)
        for c in range(x.shape[0] // chunk):
            r0 = c * chunk
            s = jnp.dot(wsg, vnb[r0:r0 + chunk, g * gd:(g + 1) * gd], preferred_element_type=F32)
            s = s + bs_ref[0][:, g * gd:(g + 1) * gd]
            gated_ref[r0:r0 + chunk, g * gd:(g + 1) * gd] = (u[r0:r0 + chunk, g * gd:(g + 1) * gd] * s).astype(BF16)
    mix = jnp.dot(gated_ref[...], wout_ref[...], preferred_element_type=F32)
    _post_mix(x, mix, g_ref, b_ref, wr_ref, br_ref, h_ref, hg_ref, route_ref, alpha, n_grp, epg)


def _gmlp_layer(x, w_in, a_g, a_b, ws2, msk2, bs2, w_out, g0, b0, wr, br, *, layer, alpha, n_grp, epg):
    nt, d = x.shape
    da = w_out.shape[1]
    n_groups, chunk = ws2.shape[1], ws2.shape[2]
    n_tiles = nt // TM
    last = n_tiles - 1
    s_n = d // LANES
    kern = functools.partial(_gmlp_kernel, n_groups=n_groups, chunk=chunk, alpha=alpha, n_grp=n_grp, epg=epg)
    sel = lambda i: (i // last, 0, 0, 0) if last > 0 else (0, 0, 0, 0)
    sel3 = lambda i: (i // last, 0, 0) if last > 0 else (0, 0, 0)
    return pl.pallas_call(
        kern,
        grid=(n_tiles,),
        in_specs=[
            pl.BlockSpec((TM, d), lambda i: (i, 0)),
            _layer_spec((d, 2 * da), (layer, 0, 0)),
            _const_spec((1, da)), _const_spec((1, da)),
            pl.BlockSpec((1, n_groups, chunk, chunk), sel),
            pl.BlockSpec((1, chunk, chunk), sel3),
            pl.BlockSpec((1, chunk, da), sel3),
            _layer_spec((da, d), (layer, 0, 0)),
            _const_spec((1, d)), _const_spec((1, d)),
            _const_spec((d, LANES)), _const_spec((1, LANES)),
        ],
        out_specs=[
            pl.BlockSpec((TM, d), lambda i: (i, 0)),
            pl.BlockSpec((TM * s_n, LANES), lambda i: (i, 0)),
            pl.BlockSpec((TM, LANES), lambda i: (i, 0)),
            pl.BlockSpec((TM, da), lambda i: (0, 0)),
        ],
        out_shape=[
            jax.ShapeDtypeStruct((nt, d), F32),
            jax.ShapeDtypeStruct((nt * s_n, LANES), F32),
            jax.ShapeDtypeStruct((nt, LANES), F32),
            jax.ShapeDtypeStruct((TM, da), F32),
        ],
        scratch_shapes=[pltpu.VMEM((TM, da), BF16)],
        compiler_params=_cparams(1),
        name="gmlp_mixer",
    )(x, w_in, a_g, a_b, ws2, msk2, bs2, w_out, g0, b0, wr, br)


def _proj_kernel(*refs, rope, outs_wanted, n_heads, rot):
    x_ref, w_ref = refs[0], refs[1]
    pos = 2
    if rope:
        cos_ref, sa_ref, sb_ref = refs[2:5]
        pos = 5
    outs = dict(zip(outs_wanted, refs[pos:]))
    y = jnp.dot(x_ref[...].astype(BF16), w_ref[...], preferred_element_type=F32)
    if rope:
        hd = y.shape[1] // n_heads
        half = rot // 2
        cos, sa, sb = cos_ref[...], sa_ref[...], sb_ref[...]
        parts = []
        for h in range(n_heads):
            yh = y[:, h * hd:(h + 1) * hd]
            parts.append(yh * cos + pltpu.roll(yh, hd - half, 1) * sa + pltpu.roll(yh, half, 1) * sb)
        y = jnp.concatenate(parts, axis=1)
    if "slabs" in outs:
        _to_slabs(outs["slabs"], y)
    if "rows" in outs:
        outs["rows"][...] = y.astype(BF16)
    if "cols" in outs:
        outs["cols"][...] = y.T.astype(BF16)
    if "mean" in outs:
        outs["mean"][0] = jnp.mean(y, axis=0, keepdims=True)


def _proj(x, w, tables, *, layer, part, rope, outs_wanted, n_heads, rot, name):
    nt, d = x.shape
    dn = d
    n_tiles = nt // TM
    s_n = dn // LANES
    kern = functools.partial(_proj_kernel, rope=rope, outs_wanted=outs_wanted, n_heads=n_heads, rot=rot)
    in_specs = [pl.BlockSpec((TM, d), lambda i: (i, 0)), _layer_spec((d, dn), (layer, 0, part))]
    args = [x, w]
    if rope:
        in_specs += [pl.BlockSpec((TM, LANES), lambda i: (i, 0))] * 3
        args += list(tables)
    specs = {
        "slabs": (pl.BlockSpec((TM * s_n, LANES), lambda i: (i, 0)), jax.ShapeDtypeStruct((nt * s_n, LANES), F32)),
        "rows": (pl.BlockSpec((TM, dn), lambda i: (i, 0)), jax.ShapeDtypeStruct((nt, dn), BF16)),
        "cols": (pl.BlockSpec((dn, TM), lambda i: (0, i)), jax.ShapeDtypeStruct((dn, nt), BF16)),
        "mean": (pl.BlockSpec((1, 1, dn), lambda i: (i, 0, 0)), jax.ShapeDtypeStruct((n_tiles, 1, dn), F32)),
    }
    return pl.pallas_call(
        kern, grid=(n_tiles,), in_specs=in_specs,
        out_specs=[specs[o][0] for o in outs_wanted], out_shape=[specs[o][1] for o in outs_wanted],
        compiler_params=_cparams(1), name=name,
    )(*args)


def _top_blocks(gate, idx, topk, axis):
    big = float(gate.shape[axis])
    ids = []
    for _ in range(topk):
        m = jnp.max(gate, axis=axis, keepdims=True)
        sel = jnp.min(jnp.where(gate == m, idx, big), axis=axis, keepdims=True)
        sel = jnp.where(m > NEG_INF, sel, -1.0)
        ids.append(sel)
        gate = jnp.where(idx == sel, NEG_INF, gate)
    return ids


def _online_softmax(s, keep, m, l, c2):
    ms, ls, scales, ps = [], [], [], []
    for q0 in range(0, s.shape[1], LANES):
        qs = slice(q0, q0 + LANES)
        sh = jnp.where(keep[:, qs], s[:, qs], NEG_INF)
        m_new = jnp.maximum(m[:, qs], jnp.max(sh, axis=0, keepdims=True))
        a = jnp.exp2((m[:, qs] - m_new) * c2)
        p = jnp.exp2((sh - m_new) * c2)
        ms.append(m_new)
        ls.append(a * l[:, qs] + jnp.sum(p, axis=0, keepdims=True))
        scales.append(a)
        ps.append(p.astype(BF16))
    cat = lambda xs: jnp.concatenate(xs, axis=1)
    return cat(ms), cat(ls), cat(scales), cat(ps)


def _moba_prompt_kernel(qt_ref, k_ref, vt_ref, km_ref, o_ref, s_buf, p_buf, acc_buf, *, blk, topk, scale, hg, hd):
    n_blk = k_ref.shape[0] // blk
    kmb = km_ref[...].astype(BF16)
    key_i = lax.broadcasted_iota(jnp.int32, (blk, blk), 0)
    qry_i = lax.broadcasted_iota(jnp.int32, (blk, blk), 1)
    causal = key_i <= qry_i
    blk_i = lax.broadcasted_iota(jnp.int32, (n_blk, blk), 0).astype(F32)
    c2 = scale * LOG2_E

    def q_tile(qi, carry):
        q0 = _aligned(qi * blk, blk)
        qf = jnp.asarray(qi, F32)
        ids, init = [], []
        for h in range(hg):
            hs = slice(h * hd, (h + 1) * hd)
            qt = qt_ref[hs, pl.ds(q0, blk)]
            s_buf[h] = jnp.dot(k_ref[pl.ds(0, blk), hs], qt, preferred_element_type=F32)
            gate = jnp.dot(kmb[:, hs], qt, preferred_element_type=F32)
            gate = jnp.where(blk_i < qf, gate, NEG_INF)
            ids.append(_top_blocks(gate, blk_i, topk, 0))
            s = jnp.dot(k_ref[pl.ds(q0, blk), hs], qt, preferred_element_type=F32)
            m, l, _, p = _online_softmax(s, causal, jnp.full((1, blk), NEG_INF, F32), jnp.zeros((1, blk), F32), c2)
            p_buf[h] = p
            acc_buf[h] = jnp.zeros((hd, blk), F32)
            init.append((m, l, jnp.zeros((1, blk), F32)))

        def kv_step(n, c):
            prev0, stats = c
            prev0 = _aligned(prev0, blk)
            nf = jnp.asarray(n, F32)
            nxt0 = _aligned(jnp.minimum(n + 1, qi - 1) * blk, blk)
            out = []
            for h in range(hg):
                hs = slice(h * hd, (h + 1) * hd)
                m, l, a_prev = stats[h]
                s = s_buf[h]
                s_buf[h] = jnp.dot(k_ref[pl.ds(nxt0, blk), hs], qt_ref[hs, pl.ds(q0, blk)], preferred_element_type=F32)
                pv = jnp.dot(vt_ref[hs, pl.ds(prev0, blk)], p_buf[h], preferred_element_type=F32)
                acc_buf[h] = a_prev * acc_buf[h] + pv
                picked = ids[h][0] == nf
                for t in range(1, topk):
                    picked = picked | (ids[h][t] == nf)
                m, l, a, p = _online_softmax(s, picked, m, l, c2)
                p_buf[h] = p
                out.append((m, l, a))
            return _aligned(n * blk, blk), tuple(out)

        prev0, fin = lax.fori_loop(0, qi, kv_step, (jnp.asarray(q0, jnp.int32), tuple(init)))
        prev0 = _aligned(prev0, blk)
        for h in range(hg):
            hs = slice(h * hd, (h + 1) * hd)
            m, l, a_prev = fin[h]
            acc = a_prev * acc_buf[h] + jnp.dot(vt_ref[hs, pl.ds(prev0, blk)], p_buf[h], preferred_element_type=F32)
            o_ref[pl.ds(q0, blk), hs] = (acc / l).T.astype(o_ref.dtype)
        return carry

    lax.fori_loop(0, n_blk, q_tile, 0)


def _moba_prompt(qt, k, vt, km2, *, bp, tp, n_heads, hd):
    d = k.shape[1]
    hg = min(MOBA_HEADS_PER_STEP, n_heads)
    n_blk = tp // MOBA_BLOCK
    kern = functools.partial(_moba_prompt_kernel, blk=MOBA_BLOCK, topk=MOBA_TOPK, scale=hd ** -0.5, hg=hg, hd=hd)
    row_spec = pl.BlockSpec((tp, hg * hd), lambda b, g: (b, g))
    col_spec = pl.BlockSpec((hg * hd, tp), lambda b, g: (g, b))
    return pl.pallas_call(
        kern, grid=(bp, n_heads // hg),
        in_specs=[col_spec, row_spec, col_spec, pl.BlockSpec((n_blk, hg * hd), lambda b, g: (b, g))],
        out_specs=row_spec,
        out_shape=jax.ShapeDtypeStruct((bp * tp, d), BF16),
        scratch_shapes=[pltpu.VMEM((hg, MOBA_BLOCK, MOBA_BLOCK), F32), pltpu.VMEM((hg, MOBA_BLOCK, MOBA_BLOCK), BF16),
                        pltpu.VMEM((hg, hd, MOBA_BLOCK), F32)],
        compiler_params=_cparams(2), name="moba_prompt",
    )(qt, k, vt, km2)


def _cache_mean_kernel(pt_ref, *refs, ppb, inv):
    o_ref = refs[ppb]
    acc = jnp.sum(refs[0][...], axis=0)
    for t in range(1, ppb):
        acc = acc + jnp.sum(refs[t][...], axis=0)
    o_ref[0, 0] = acc * inv


def _cache_means(cache_k, pt_flat, *, bs, n_pages):
    _, page, nbl, n_heads, hd = cache_k.shape
    ppb = MOBA_BLOCK // page
    n_cblk = n_pages // ppb
    kern = functools.partial(_cache_mean_kernel, ppb=ppb, inv=1.0 / MOBA_BLOCK)

    def page_spec(t):
        return pl.BlockSpec((None, page, nbl, n_heads, hd),
                            lambda b, n, pt: (pt[b * n_pages + n * ppb + t], 0, 0, 0, 0))

    return pl.pallas_call(
        kern,
        grid_spec=pltpu.PrefetchScalarGridSpec(
            num_scalar_prefetch=1, grid=(bs, n_cblk),
            in_specs=[page_spec(t) for t in range(ppb)],
            out_specs=pl.BlockSpec((1, 1, nbl, n_heads, hd), lambda b, n, pt: (b, n, 0, 0, 0)),
        ),
        out_shape=jax.ShapeDtypeStruct((bs, n_cblk, nbl, n_heads, hd), F32),
        compiler_params=_cparams(2), name="cache_block_means",
    )(pt_flat, *([cache_k] * ppb))


def _sample_gate_kernel(q_ref, km_ref, o_ref, *, topk):
    n_heads = q_ref.shape[1]
    n_cblk = km_ref.shape[2]
    nt_dims = (((1,), (1,)), ((), ()))
    lane = lax.broadcasted_iota(jnp.int32, (Q_PAD, n_cblk), 1).astype(F32)
    out_lane = lax.broadcasted_iota(jnp.int32, (Q_PAD, LANES), 1)
    for h in range(n_heads):
        gate = lax.dot_general(q_ref[0, h], km_ref[0, h].astype(BF16), nt_dims, preferred_element_type=F32)
        ids = _top_blocks(gate, lane, topk, 1)
        out = jnp.zeros((Q_PAD, LANES), F32)
        for t in range(topk):
            out = jnp.where(out_lane == t, ids[t], out)
        o_ref[0, h] = out.astype(jnp.int32)


def _sample_gate(q_s, km_s):
    bs, n_heads, _, hd = q_s.shape
    n_cblk = km_s.shape[2]
    kern = functools.partial(_sample_gate_kernel, topk=MOBA_TOPK)
    return pl.pallas_call(
        kern, grid=(bs,),
        in_specs=[pl.BlockSpec((1, n_heads, Q_PAD, hd), lambda b: (b, 0, 0, 0)),
                  pl.BlockSpec((1, n_heads, n_cblk, hd), lambda b: (b, 0, 0, 0))],
        out_specs=pl.BlockSpec((1, n_heads, Q_PAD, LANES), lambda b: (b, 0, 0, 0)),
        out_shape=jax.ShapeDtypeStruct((bs, n_heads, Q_PAD, LANES), jnp.int32),
        compiler_params=_cparams(1), name="sample_gate",
    )(q_s, km_s)


def _sample_attn_kernel(pt_ref, ids_ref, ck_hbm, cv_hbm, q_ref, kn_ref, vn_ref, o_ref, kbuf, vbuf, sem,
                        *, layer, ts, topk, ppb, n_pages, scale):
    b, h = pl.program_id(0), pl.program_id(1)
    n_heads = pl.num_programs(1)
    step = b * n_heads + h
    n_steps = pl.num_programs(0) * n_heads
    page = kbuf.shape[2]

    def copies(bb, hh, slot):
        out = []
        for qi in range(ts):
            for t in range(topk):
                blk = ids_ref[((bb * n_heads + hh) * ts + qi) * topk + t]
                for half in range(ppb):
                    pg = pt_ref[bb * n_pages + blk * ppb + half]
                    j = (qi * topk + t) * ppb + half
                    out.append(pltpu.make_async_copy(ck_hbm.at[pg, :, layer, hh, :], kbuf.at[slot, j], sem.at[0, slot]))
                    out.append(pltpu.make_async_copy(cv_hbm.at[pg, :, layer, hh, :], vbuf.at[slot, j], sem.at[1, slot]))
        return out

    @pl.when(step == 0)
    def _():
        for c in copies(b, h, 0):
            c.start()

    @pl.when(step + 1 < n_steps)
    def _():
        nxt = step + 1
        for c in copies(nxt // n_heads, nxt % n_heads, nxt % 2):
            c.start()

    slot = step % 2
    for c in copies(b, h, slot):
        c.wait()

    nt_dims = (((1,), (1,)), ((), ()))
    q = q_ref[0, 0]
    rows = lax.broadcasted_iota(jnp.int32, (Q_PAD, Q_PAD), 0)
    cols = lax.broadcasted_iota(jnp.int32, (Q_PAD, Q_PAD), 1)
    s_new = lax.dot_general(q, kn_ref[0, 0], nt_dims, preferred_element_type=F32) * scale
    s_new = jnp.where((cols <= rows) & (cols < ts), s_new, NEG_INF)
    out_rows = lax.broadcasted_iota(jnp.int32, (Q_PAD, q.shape[1]), 0)
    out = jnp.zeros((Q_PAD, q.shape[1]), F32)
    per_q = topk * ppb
    for qi in range(ts):
        kq = kbuf[slot, qi * per_q:(qi + 1) * per_q].reshape(per_q * page, -1).astype(BF16)
        vq = vbuf[slot, qi * per_q:(qi + 1) * per_q].reshape(per_q * page, -1).astype(BF16)
        s = lax.dot_general(q, kq, nt_dims, preferred_element_type=F32) * scale
        m = jnp.maximum(jnp.max(s, axis=1, keepdims=True), jnp.max(s_new, axis=1, keepdims=True))
        p = jnp.exp(s - m)
        pn = jnp.exp(s_new - m)
        l = jnp.sum(p, axis=1, keepdims=True) + jnp.sum(pn, axis=1, keepdims=True)
        o = jnp.dot(p.astype(BF16), vq, preferred_element_type=F32)
        o = o + jnp.dot(pn.astype(BF16), vn_ref[0, 0], preferred_element_type=F32)
        out = jnp.where(out_rows == qi, o / l, out)
    o_ref[0, 0] = out


def _sample_attn(pt_flat, ids_flat, cache_k, cache_v, q_s, kn_s, vn_s, *, layer, ts, n_pages):
    bs, n_heads, _, hd = q_s.shape
    page = cache_k.shape[1]
    ppb = MOBA_BLOCK // page
    n_sel = ts * MOBA_TOPK * ppb
    kern = functools.partial(_sample_attn_kernel, layer=layer, ts=ts, topk=MOBA_TOPK, ppb=ppb,
                             n_pages=n_pages, scale=hd ** -0.5)
    qspec = pl.BlockSpec((1, 1, Q_PAD, hd), lambda b, h, pt, ids: (b, h, 0, 0))
    return pl.pallas_call(
        kern,
        grid_spec=pltpu.PrefetchScalarGridSpec(
            num_scalar_prefetch=2, grid=(bs, n_heads),
            in_specs=[pl.BlockSpec(memory_space=pl.ANY), pl.BlockSpec(memory_space=pl.ANY), qspec, qspec, qspec],
            out_specs=qspec,
            scratch_shapes=[pltpu.VMEM((2, n_sel, page, hd), F32), pltpu.VMEM((2, n_sel, page, hd), F32),
                            pltpu.SemaphoreType.DMA((2, 2))],
        ),
        out_shape=jax.ShapeDtypeStruct((bs, n_heads, Q_PAD, hd), F32),
        compiler_params=_cparams(2), name="sample_attn",
    )(pt_flat, ids_flat, cache_k, cache_v, q_s, kn_s, vn_s)


def _oproj_kernel(a_ref, tail_ref, x_ref, wo_ref, g_ref, b_ref, wr_ref, br_ref, h_ref, hg_ref, route_ref,
                  *, alpha, n_grp, epg):
    is_tail = pl.program_id(0) == pl.num_programs(0) - 1
    a = jnp.where(is_tail, tail_ref[...], a_ref[...])
    mix = jnp.dot(a, wo_ref[...], preferred_element_type=F32)
    _post_mix(x_ref[...], mix, g_ref, b_ref, wr_ref, br_ref, h_ref, hg_ref, route_ref, alpha, n_grp, epg)


def _oproj_layer(attn, tail, x, w_o, g0, b0, wr, br, *, layer, alpha, n_grp, epg):
    nt, d = x.shape
    s_n = d // LANES
    last_prompt = attn.shape[0] // TM - 1
    kern = functools.partial(_oproj_kernel, alpha=alpha, n_grp=n_grp, epg=epg)
    return pl.pallas_call(
        kern, grid=(nt // TM,),
        in_specs=[pl.BlockSpec((TM, d), lambda i: (jnp.minimum(i, last_prompt), 0)),
                  pl.BlockSpec((TM, d), lambda i: (0, 0)),
                  pl.BlockSpec((TM, d), lambda i: (i, 0)),
                  _layer_spec((d, d), (layer, 0, 0)), _const_spec((1, d)), _const_spec((1, d)),
                  _const_spec((d, LANES)), _const_spec((1, LANES))],
        out_specs=[pl.BlockSpec((TM, d), lambda i: (i, 0)),
                   pl.BlockSpec((TM * s_n, LANES), lambda i: (i, 0)),
                   pl.BlockSpec((TM, LANES), lambda i: (i, 0))],
        out_shape=[jax.ShapeDtypeStruct((nt, d), F32),
                   jax.ShapeDtypeStruct((nt * s_n, LANES), F32),
                   jax.ShapeDtypeStruct((nt, LANES), F32)],
        compiler_params=_cparams(1), name="attn_out_proj",
    )(attn, tail, x, w_o, g0, b0, wr, br)


GATHER_UNROLL = 16
FINISH_CHUNKS = 8


def _start_row_gather(idx_ref, idx_base, n_rows, src_hbm, dst, dst_base, s_n, sem):
    def body(rr, c):
        for k in range(GATHER_UNROLL):
            r = rr * GATHER_UNROLL + k
            row = idx_ref[idx_base + r]
            d0 = _aligned((dst_base + r) * s_n, s_n)
            pltpu.make_async_copy(src_hbm.at[row], dst.at[pl.ds(d0, s_n)], sem).start()
        return c
    lax.fori_loop(0, n_rows // GATHER_UNROLL, body, 0)


def _start_row_scatter(idx_ref, idx_base, n_rows, src, dst_hbm, s_n, sem):
    def body(rr, c):
        for k in range(GATHER_UNROLL):
            r = rr * GATHER_UNROLL + k
            s0 = _aligned(r * s_n, s_n)
            d0 = _aligned(idx_ref[idx_base + r] * s_n, s_n)
            pltpu.make_async_copy(src.at[pl.ds(s0, s_n)], dst_hbm.at[pl.ds(d0, s_n)], sem).start()
        return c
    lax.fori_loop(0, n_rows // GATHER_UNROLL, body, 0)


def _moe_plan(route, n_exp):
    nt = route.shape[0]
    e_flat = route[:, :2].astype(jnp.int32).T.reshape(-1)
    ar = jnp.arange(n_exp, dtype=jnp.int32)
    onehot = (e_flat[:, None] == ar[None, :]).astype(jnp.int32)
    csum = jnp.cumsum(onehot, axis=0)
    counts = csum[-1]
    padded = ((counts + TM - 1) // TM) * TM
    pend = jnp.cumsum(padded)
    pstart = pend - padded
    pos = jnp.sum(onehot * (csum - 1 + pstart[None, :]), axis=1)
    n_tiles = (2 * nt + n_exp * TM) // TM
    tile_start = jnp.arange(n_tiles, dtype=jnp.int32) * TM
    tile_e = jnp.sum((tile_start[:, None] >= pend[None, :]).astype(jnp.int32), axis=1)
    tile_e = jnp.minimum(tile_e, jnp.max(jnp.where(counts > 0, ar, 0)))
    n_used = (pend[-1] // TM).reshape(1)
    i32 = lambda a: a.astype(jnp.int32)
    return i32(pos), i32(tile_e), i32(n_used), i32(pend), i32(padded - counts)


def _dispatch_kernel(pos_ref, pend_ref, npad_ref, hg_ref, xs_hbm, zbuf, sem, zsem, *, nt, n_exp, s_n):
    i = pl.program_id(0)
    n_rows = xs_hbm.shape[0] // s_n
    rows = hg_ref.shape[0] // s_n

    def zero_rows(first, count):
        def body(r, c):
            d0 = _aligned((first + r) * s_n, s_n)
            pltpu.make_async_copy(zbuf, xs_hbm.at[pl.ds(d0, s_n)], zsem).start()
            return c
        lax.fori_loop(0, count, body, 0)

    @pl.when(i == 0)
    def _():
        zbuf[...] = jnp.zeros_like(zbuf)
        for e in range(n_exp):
            zero_rows(pend_ref[e] - npad_ref[e], npad_ref[e])
        zero_rows(pend_ref[n_exp - 1], n_rows - pend_ref[n_exp - 1])

    for k in range(2):
        _start_row_scatter(pos_ref, k * nt + i * rows, rows, hg_ref, xs_hbm, s_n, sem)
    for k in range(2):
        pltpu.make_async_copy(hg_ref, xs_hbm.at[pl.ds(0, rows * s_n)], sem).wait()

    @pl.when(i == pl.num_programs(0) - 1)
    def _():
        n_zero = n_exp * TM * s_n
        pltpu.make_async_copy(xs_hbm.at[pl.ds(0, n_zero)], xs_hbm.at[pl.ds(0, n_zero)], zsem).wait()


def _dispatch(hg, pos, pend, npad, n_rows, s_n):
    nt = hg.shape[0] // s_n
    n_exp = pend.shape[0]
    kern = functools.partial(_dispatch_kernel, nt=nt, n_exp=n_exp, s_n=s_n)
    return pl.pallas_call(
        kern,
        grid_spec=pltpu.PrefetchScalarGridSpec(
            num_scalar_prefetch=3, grid=(nt // TM,),
            in_specs=[pl.BlockSpec((TM * s_n, LANES), lambda i, pos, pend, npad: (i, 0))],
            out_specs=pl.BlockSpec(memory_space=pl.ANY),
            scratch_shapes=[pltpu.VMEM((s_n, LANES), F32), pltpu.SemaphoreType.DMA(()), pltpu.SemaphoreType.DMA(())],
        ),
        out_shape=jax.ShapeDtypeStruct((n_rows * s_n, LANES), F32),
        compiler_params=_cparams(1), name="moe_dispatch",
    )(pos, pend, npad, hg)


def _moe_kernel(te_ref, nu_ref, x_ref, wg_ref, wu_ref, wd_ref, y_ref, wgb, wub, wdb):
    i = pl.program_id(0)
    s_n = wg_ref.shape[1] // LANES
    rows = x_ref.shape[0] // s_n

    @pl.when(i < nu_ref[0])
    def _():
        @pl.when((i == 0) | (te_ref[i] != te_ref[jnp.maximum(i - 1, 0)]))
        def _():
            wgb[...] = wg_ref[0].astype(BF16)
            wub[...] = wu_ref[0].astype(BF16)
            wdb[...] = wd_ref[0].astype(BF16)

        x = _from_slabs(x_ref, 0, rows, s_n).astype(BF16)
        a = jnp.dot(x, wgb[...], preferred_element_type=F32)
        u = jnp.dot(x, wub[...], preferred_element_type=F32)
        hid = (a * jax.nn.sigmoid(a)) * u
        _to_slabs(y_ref, jnp.dot(hid.astype(BF16), wdb[...], preferred_element_type=F32))

    @pl.when(i >= nu_ref[0])
    def _():
        y_ref[...] = jnp.zeros_like(y_ref)


def _moe(xs2, tile_e, n_used, w_gate, w_up, w_down, layer):
    n_tiles = tile_e.shape[0]
    _, _, d, f = w_gate.shape
    s_n = d // LANES
    tile = lambda i, te, nu: (jnp.minimum(i, nu[0] - 1), 0)
    wspec = lambda shape: pl.BlockSpec((None, 1) + shape, lambda i, te, nu: (layer, te[i], 0, 0))
    return pl.pallas_call(
        _moe_kernel,
        grid_spec=pltpu.PrefetchScalarGridSpec(
            num_scalar_prefetch=2, grid=(n_tiles,),
            in_specs=[pl.BlockSpec((TM * s_n, LANES), tile), wspec((d, f)), wspec((d, f)), wspec((f, d))],
            out_specs=pl.BlockSpec((TM * s_n, LANES), lambda i, te, nu: (i, 0)),
            scratch_shapes=[pltpu.VMEM((d, f), BF16), pltpu.VMEM((d, f), BF16), pltpu.VMEM((f, d), BF16)],
        ),
        out_shape=jax.ShapeDtypeStruct((n_tiles * TM * s_n, LANES), F32),
        compiler_params=_cparams(1), name="moe_experts",
    )(tile_e, n_used, xs2, w_gate, w_up, w_down)


def _finish_kernel(pos_ref, h_ref, r_ref, ys_hbm, p_ref, wpg_ref, bpg_ref, wpp_ref, g1_ref, b1_ref, g2_ref, b2_ref,
                   o_ref, ybuf, sem, *, alpha, nt):
    i = pl.program_id(0)
    n = pl.num_programs(0)
    s_n = ys_hbm.shape[1]
    rows = h_ref.shape[0]

    @pl.when(i == 0)
    def _():
        for k in range(2):
            _start_row_gather(pos_ref, k * nt, rows, ys_hbm, ybuf.at[0], k * rows, s_n, sem.at[0])

    slot = i % 2
    pltpu.make_async_copy(ybuf.at[slot], ybuf.at[slot], sem.at[slot]).wait()
    y2 = ybuf.at[slot]
    r = r_ref[...]
    moe = r[:, 2:3] * _from_slabs(y2, 0, rows, s_n) + r[:, 3:4] * _from_slabs(y2, rows * s_n, rows, s_n)
    h2 = _ln(alpha * h_ref[...] + moe, g1_ref[...], b1_ref[...])
    h2b = h2.astype(BF16)
    pb = p_ref[...].astype(BF16)

    nxt = jnp.minimum(i + 1, n - 1)
    other = ybuf.at[1 - slot]
    per = 2 * rows // FINISH_CHUNKS
    cw = wpg_ref.shape[1] // FINISH_CHUNKS
    zs = []
    for c in range(FINISH_CHUNKS):
        for q in range(c * per, (c + 1) * per):
            k, rr = divmod(q, rows)
            row = pos_ref[k * nt + nxt * rows + rr]
            pltpu.make_async_copy(ys_hbm.at[row], other.at[pl.ds(q * s_n, s_n)], sem.at[1 - slot]).start()
        cs = slice(c * cw, (c + 1) * cw)
        zg = jnp.dot(h2b, wpg_ref[:, cs], preferred_element_type=F32) + bpg_ref[:, cs]
        zp = jnp.dot(pb, wpp_ref[:, cs], preferred_element_type=F32)
        zs.append(jax.nn.sigmoid(zg) * zp)
    z = jnp.concatenate(zs, axis=1)
    o_ref[...] = h2 + _ln(z, g2_ref[...], b2_ref[...])

    @pl.when(i == n - 1)
    def _():
        pltpu.make_async_copy(other, other, sem.at[1 - slot]).wait()


def _finish(pos, h, route, ys3, p, w_pg, b_pg, w_pp, g1, b1, g2, b2, *, layer, alpha):
    nt, d = h.shape
    dp = p.shape[1]
    s_n = d // LANES
    kern = functools.partial(_finish_kernel, alpha=alpha, nt=nt)
    tile = lambda i, pos: (i, 0)
    const = _const_spec
    return pl.pallas_call(
        kern,
        grid_spec=pltpu.PrefetchScalarGridSpec(
            num_scalar_prefetch=1, grid=(nt // TM,),
            in_specs=[pl.BlockSpec((TM, d), tile), pl.BlockSpec((TM, LANES), tile), pl.BlockSpec(memory_space=pl.ANY),
                      pl.BlockSpec((TM, dp), tile), _layer_spec((d, d), (layer, 0, 0)), const((1, d)),
                      _layer_spec((dp, d), (layer, 0, 0)),
                      const((1, d)), const((1, d)), const((1, d)), const((1, d))],
            out_specs=pl.BlockSpec((TM, d), tile),
            scratch_shapes=[pltpu.VMEM((2, 2 * TM * s_n, LANES), F32), pltpu.SemaphoreType.DMA((2,))],
        ),
        out_shape=jax.ShapeDtypeStruct((nt, d), F32),
        compiler_params=_cparams(1), name="layer_finish",
    )(pos, h, route, ys3, p, w_pg, b_pg, w_pp, g1, b1, g2, b2)


def _rope_tables(positions, hd, rot):
    half = rot // 2
    inv = ROPE_THETA ** (-jnp.arange(half, dtype=F32) / half)
    ang = positions.astype(F32)[:, None] * inv[None, :]
    cos, sin = jnp.cos(ang), jnp.sin(ang)
    n = positions.shape[0]
    ones = jnp.ones((n, hd - rot), F32)
    zeros_h = jnp.zeros((n, half), F32)
    zeros_r = jnp.zeros((n, hd - rot), F32)
    cos_t = jnp.concatenate([cos, cos, ones], axis=1)
    sa = jnp.concatenate([-sin, zeros_h, zeros_r], axis=1)
    sb = jnp.concatenate([zeros_h, sin, zeros_r], axis=1)
    return cos_t, sa, sb


def kernel(x_prompt, x_sample, cache_k, cache_v, page_table, p_prompt, p_sample, a_w_in, a_ln_g, a_ln_b, a_w_s, a_b_s, a_w_out, b_w_qkv, b_w_o, ln_g, ln_b, moe_w_rc, moe_b_rc, moe_w_rf, moe_b_rf, moe_w_gate, moe_w_up, moe_w_down, ple_w_proj, ple_w_gate, ple_b_gate):
    bp, tp, d = x_prompt.shape
    bs, ts, _ = x_sample.shape
    depth = ln_g.shape[0]
    _, page, nbl, n_heads, hd = cache_k.shape
    n_pages = page_table.shape[1]
    past_len = n_pages * page
    n_groups, chunk = a_w_s.shape[1], a_w_s.shape[2]
    da = a_w_out.shape[1]
    n_grp, epg = moe_w_rf.shape[2], moe_w_rf.shape[3]
    n_exp = n_grp * epg
    rot = hd // 4
    alpha = (2.0 * depth) ** 0.25
    np_rows, ns_rows = bp * tp, bs * ts
    nt = np_rows + TM
    s_n = d // LANES

    assert tp % MOBA_BLOCK == 0 and past_len % MOBA_BLOCK == 0 and MOBA_BLOCK % page == 0
    assert ts <= min(chunk, Q_PAD) and chunk % ts == 0 and ns_rows <= chunk and TM % chunk == 0
    assert n_heads * hd == d and hd == LANES and n_grp + n_exp <= LANES and n_pages // (MOBA_BLOCK // page) >= MOBA_TOPK

    def stream(prompt, sample):
        width = prompt.shape[-1]
        pad = jnp.zeros((TM - ns_rows, width), prompt.dtype)
        return jnp.concatenate([prompt.reshape(np_rows, width), sample.reshape(ns_rows, width), pad], axis=0)

    x = stream(x_prompt, x_sample)
    pt_flat = page_table.reshape(-1).astype(jnp.int32)

    positions = jnp.concatenate([jnp.tile(jnp.arange(tp, dtype=jnp.int32), bp),
                                 jnp.tile(past_len + jnp.arange(ts, dtype=jnp.int32), bs),
                                 jnp.zeros((TM - ns_rows,), jnp.int32)])
    tables = _rope_tables(positions, hd, rot)

    r = np.arange(chunk)
    tril = (r[None, :] <= r[:, None])
    same_seq = (r[None, :] // ts == r[:, None] // ts) & (r[:, None] < ns_rows) & (r[None, :] < ns_rows)
    msk2 = jnp.asarray(np.stack([tril, tril & same_seq]).astype(np.float32))

    km_cache = _cache_means(cache_k, pt_flat, bs=bs, n_pages=n_pages)
    n_sorted = 2 * nt + n_exp * TM

    w_in_b, w_out_b = a_w_in.astype(BF16), a_w_out.astype(BF16)
    w_qkv_b, w_o_b = b_w_qkv.astype(BF16), b_w_o.astype(BF16)
    w_pg_b, w_pp_b = ple_w_gate.astype(BF16), ple_w_proj.astype(BF16)

    k_rows, v_rows, chunk_rows = [], [], []
    for i in range(depth):
        j = i // 2
        wr = jnp.concatenate([moe_w_rc[i], moe_w_rf[i].reshape(d, n_exp),
                              jnp.zeros((d, LANES - n_grp - n_exp), F32)], axis=1).astype(BF16)
        br = jnp.concatenate([moe_b_rc[i], moe_b_rf[i].reshape(n_exp),
                              jnp.zeros((LANES - n_grp - n_exp,), F32)]).reshape(1, LANES)
        g0, b0 = ln_g[i, 0].reshape(1, d), ln_b[i, 0].reshape(1, d)
        if i % 2 == 0:
            ws_s = jnp.tile(a_w_s[j][:, :ts, :ts], (1, chunk // ts, chunk // ts))
            ws2 = jnp.stack([a_w_s[j], ws_s])
            bs_p = jnp.repeat(a_b_s[j].T, da // n_groups, axis=1)
            bs_s = jnp.repeat(jnp.tile(a_b_s[j][:, :ts], (1, chunk // ts)).T, da // n_groups, axis=1)
            bs2 = jnp.stack([bs_p, bs_s])
            h, hg, route, vn = _gmlp_layer(
                x, w_in_b, a_ln_g[j].reshape(1, da), a_ln_b[j].reshape(1, da),
                ws2, msk2, bs2, w_out_b, g0, b0, wr, br, layer=j, alpha=alpha, n_grp=n_grp, epg=epg)
            chunk_rows.append(vn[:ns_rows].reshape(bs, ts, da))
        else:
            proj = functools.partial(_proj, x, w_qkv_b, layer=j, n_heads=n_heads, rot=rot)
            (q_t,) = proj(tables, part=0, rope=True, outs_wanted=("cols",), name="q_proj")
            k_sl, k_bf, km = proj(tables, part=1, rope=True, outs_wanted=("slabs", "rows", "mean"), name="k_proj")
            v_sl, v_t = proj(None, part=2, rope=False, outs_wanted=("slabs", "cols"), name="v_proj")
            k_sl, v_sl = k_sl.reshape(nt, n_heads, hd), v_sl.reshape(nt, n_heads, hd)
            v_tl = v_sl[np_rows:np_rows + ns_rows]
            k_rows.append((k_sl[:np_rows], k_sl[np_rows:np_rows + ns_rows]))
            v_rows.append((v_sl[:np_rows], v_tl))
            attn = _moba_prompt(q_t, k_bf, v_t, km.reshape(nt // TM, d), bp=bp, tp=tp, n_heads=n_heads, hd=hd)

            def heads(a):
                a = a.reshape(bs, ts, n_heads, hd).transpose(0, 2, 1, 3).astype(BF16)
                return jnp.pad(a, ((0, 0), (0, 0), (0, Q_PAD - ts), (0, 0)))

            sl = slice(np_rows, np_rows + ns_rows)
            q_s, kn_s, vn_s = heads(q_t[:, sl].T), heads(k_bf[sl]), heads(v_tl)
            ids = _sample_gate(q_s, km_cache[:, :, j].transpose(0, 2, 1, 3))
            ids_flat = ids[:, :, :ts, :MOBA_TOPK].reshape(-1)
            o_s = _sample_attn(pt_flat, ids_flat, cache_k, cache_v, q_s, kn_s, vn_s, layer=j, ts=ts, n_pages=n_pages)
            o_s = o_s[:, :, :ts].transpose(0, 2, 1, 3).reshape(ns_rows, d).astype(BF16)
            tail = jnp.concatenate([o_s, jnp.zeros((TM - ns_rows, d), BF16)], axis=0)
            h, hg, route = _oproj_layer(attn, tail, x, w_o_b, g0, b0, wr, br,
                                        layer=j, alpha=alpha, n_grp=n_grp, epg=epg)
        pos, tile_e, n_used, pend, npad = _moe_plan(route, n_exp)
        xs = _dispatch(hg, pos, pend, npad, n_sorted, s_n)
        ys = _moe(xs, tile_e, n_used, moe_w_gate, moe_w_up, moe_w_down, i)
        p = stream(p_prompt[i], p_sample[i])
        x = _finish(pos, h, route, ys.reshape(n_sorted, s_n, LANES), p, w_pg_b,
                    ple_b_gate[i].reshape(1, d), w_pp_b,
                    ln_g[i, 1].reshape(1, d), ln_b[i, 1].reshape(1, d),
                    ln_g[i, 2].reshape(1, d), ln_b[i, 2].reshape(1, d), layer=i, alpha=alpha)

    def prompt_rows(rows):
        return jnp.stack([a for a, _ in rows], axis=1).reshape(bp, tp, nbl, n_heads, hd)

    def sample_rows(rows):
        return jnp.stack([a for _, a in rows], axis=1).reshape(bs, ts, nbl, n_heads, hd)

    y_prompt = x[:np_rows].reshape(bp, tp, d)
    y_sample = x[np_rows:np_rows + ns_rows].reshape(bs, ts, d)
    chunk_v = jnp.stack(chunk_rows, axis=2)
    return (y_prompt, y_sample, prompt_rows(k_rows), prompt_rows(v_rows),
            sample_rows(k_rows), sample_rows(v_rows), chunk_v)
```

```python
import functools

import numpy as np
import jax
import jax.numpy as jnp
from jax import lax
from jax.experimental import pallas as pl
from jax.experimental.pallas import tpu as pltpu

F32 = jnp.float32
BF16 = jnp.bfloat16

LANES = 128
TM = 256
MOBA_BLOCK = 256
MOBA_TOPK = 3
MOBA_HEADS_PER_STEP = 4
ROPE_THETA = 500000.0
LN_EPS = 1e-5
Q_PAD = 16
VMEM_LIMIT = 56 * 1024 * 1024
NEG_INF = float("-inf")
LOG2_E = 1.4426950408889634


def _cparams(n_axes):
    return pltpu.CompilerParams(dimension_semantics=("arbitrary",) * n_axes,
                                vmem_limit_bytes=VMEM_LIMIT)


def _const_spec(shape):
    zeros = (0,) * len(shape)
    return pl.BlockSpec(shape, lambda *a: zeros, pipeline_mode=pl.Buffered(1))


def _layer_spec(shape, idx):
    return pl.BlockSpec((None,) + shape, lambda *a: idx, pipeline_mode=pl.Buffered(1))


def _aligned(x, m):
    return x if isinstance(x, int) else pl.multiple_of(x, m)


def _ln(x, g, b):
    mu = jnp.mean(x, axis=-1, keepdims=True)
    xc = x - mu
    var = jnp.mean(xc * xc, axis=-1, keepdims=True)
    return xc * lax.rsqrt(var + LN_EPS) * g + b


def _to_slabs(ref, val):
    rows, d = val.shape
    s_n = d // LANES
    for s in range(s_n):
        ref[pl.ds(s, rows, stride=s_n), :] = val[:, s * LANES:(s + 1) * LANES]


def _from_slabs(ref, base, rows, s_n):
    cols = [ref[pl.ds(base + s, rows, stride=s_n), :] for s in range(s_n)]
    return jnp.concatenate(cols, axis=1)


def _route(h, wr_ref, br_ref, n_grp, epg):
    logits = jnp.dot(h.astype(BF16), wr_ref[...], preferred_element_type=F32) + br_ref[...]
    lane = lax.broadcasted_iota(jnp.int32, logits.shape, 1).astype(F32)
    big = float(LANES)
    lc = jnp.where(lane < n_grp, logits, NEG_INF)
    mc = jnp.max(lc, axis=1, keepdims=True)
    zc = jnp.sum(jnp.exp(lc - mc), axis=1, keepdims=True)
    pg = 1.0 / zc
    g = jnp.min(jnp.where(lc == mc, lane, big), axis=1, keepdims=True)
    lo = n_grp + g * epg
    lf = jnp.where((lane >= lo) & (lane < lo + epg), logits, NEG_INF)
    m1 = jnp.max(lf, axis=1, keepdims=True)
    i1 = jnp.min(jnp.where(lf == m1, lane, big), axis=1, keepdims=True)
    lf2 = jnp.where(lane == i1, NEG_INF, lf)
    m2 = jnp.max(lf2, axis=1, keepdims=True)
    i2 = jnp.min(jnp.where(lf2 == m2, lane, big), axis=1, keepdims=True)
    e2 = jnp.exp(m2 - m1)
    w1 = pg / (1.0 + e2)
    w2 = pg * e2 / (1.0 + e2)
    out = jnp.where(lane == 0, i1 - n_grp, 0.0)
    out = jnp.where(lane == 1, i2 - n_grp, out)
    out = jnp.where(lane == 2, w1, out)
    out = jnp.where(lane == 3, w2, out)
    return out


def _post_mix(x, mix, g_ref, b_ref, wr_ref, br_ref, h_ref, hg_ref, route_ref, alpha, n_grp, epg):
    h = _ln(alpha * x + mix, g_ref[...], b_ref[...])
    h_ref[...] = h
    _to_slabs(hg_ref, h)
    route_ref[...] = _route(h, wr_ref, br_ref, n_grp, epg)


def _gmlp_kernel(x_ref, win_ref, ag_ref, ab_ref, ws_ref, msk_ref, bs_ref, wout_ref,
                 g_ref, b_ref, wr_ref, br_ref,
                 h_ref, hg_ref, route_ref, vn_ref, gated_ref,
                 *, n_groups, chunk, alpha, n_grp, epg):
    x = x_ref[...]
    z = jnp.dot(x.astype(BF16), win_ref[...], preferred_element_type=F32)
    z = jax.nn.gelu(z)
    da = z.shape[1] // 2
    u = z[:, :da]
    vn = _ln(z[:, da:], ag_ref[...], ab_ref[...])

    @pl.when(pl.program_id(0) == pl.num_programs(0) - 1)
    def _():
        vn_ref[...] = vn

    vnb = vn.astype(BF16)
    gd = da // n_groups
    mask = msk_ref[0] > 0.0
    for g in range(n_groups):
        wsg = jnp.where(mask, ws_ref[0, g], 0.0).astype(BF16)
        for c in range(x.shape[0] // chunk):
            r0 = c * chunk
            s = jnp.dot(wsg, vnb[r0:r0 + chunk, g * gd:(g + 1) * gd], preferred_element_type=F32)
            s = s + bs_ref[0][:, g * gd:(g + 1) * gd]
            gated_ref[r0:r0 + chunk, g * gd:(g + 1) * gd] = (u[r0:r0 + chunk, g * gd:(g + 1) * gd] * s).astype(BF16)
    mix = jnp.dot(gated_ref[...], wout_ref[...], preferred_element_type=F32)
    _post_mix(x, mix, g_ref, b_ref, wr_ref, br_ref, h_ref, hg_ref, route_ref, alpha, n_grp, epg)


def _gmlp_layer(x, w_in, a_g, a_b, ws2, msk2, bs2, w_out, g0, b0, wr, br, *, layer, alpha, n_grp, epg):
    nt, d = x.shape
    da = w_out.shape[1]
    n_groups, chunk = ws2.shape[1], ws2.shape[2]
    n_tiles = nt // TM
    last = n_tiles - 1
    s_n = d // LANES
    kern = functools.partial(_gmlp_kernel, n_groups=n_groups, chunk=chunk, alpha=alpha, n_grp=n_grp, epg=epg)
    sel = lambda i: (i // last, 0, 0, 0) if last > 0 else (0, 0, 0, 0)
    sel3 = lambda i: (i // last, 0, 0) if last > 0 else (0, 0, 0)
    return pl.pallas_call(
        kern,
        grid=(n_tiles,),
        in_specs=[
            pl.BlockSpec((TM, d), lambda i: (i, 0)),
            _layer_spec((d, 2 * da), (layer, 0, 0)),
            _const_spec((1, da)), _const_spec((1, da)),
            pl.BlockSpec((1, n_groups, chunk, chunk), sel),
            pl.BlockSpec((1, chunk, chunk), sel3),
            pl.BlockSpec((1, chunk, da), sel3),
            _layer_spec((da, d), (layer, 0, 0)),
            _const_spec((1, d)), _const_spec((1, d)),
            _const_spec((d, LANES)), _const_spec((1, LANES)),
        ],
        out_specs=[
            pl.BlockSpec((TM, d), lambda i: (i, 0)),
            pl.BlockSpec((TM * s_n, LANES), lambda i: (i, 0)),
            pl.BlockSpec((TM, LANES), lambda i: (i, 0)),
            pl.BlockSpec((TM, da), lambda i: (0, 0)),
        ],
        out_shape=[
            jax.ShapeDtypeStruct((nt, d), F32),
            jax.ShapeDtypeStruct((nt * s_n, LANES), F32),
            jax.ShapeDtypeStruct((nt, LANES), F32),
            jax.ShapeDtypeStruct((TM, da), F32),
        ],
        scratch_shapes=[pltpu.VMEM((TM, da), BF16)],
        compiler_params=_cparams(1),
        name="gmlp_mixer",
    )(x, w_in, a_g, a_b, ws2, msk2, bs2, w_out, g0, b0, wr, br)


def _proj_kernel(*refs, rope, outs_wanted, n_heads, rot):
    x_ref, w_ref = refs[0], refs[1]
    pos = 2
    if rope:
        cos_ref, sa_ref, sb_ref = refs[2:5]
        pos = 5
    outs = dict(zip(outs_wanted, refs[pos:]))
    y = jnp.dot(x_ref[...].astype(BF16), w_ref[...], preferred_element_type=F32)
    if rope:
        hd = y.shape[1] // n_heads
        half = rot // 2
        cos, sa, sb = cos_ref[...], sa_ref[...], sb_ref[...]
        parts = []
        for h in range(n_heads):
            yh = y[:, h * hd:(h + 1) * hd]
            parts.append(yh * cos + pltpu.roll(yh, hd - half, 1) * sa + pltpu.roll(yh, half, 1) * sb)
        y = jnp.concatenate(parts, axis=1)
    if "slabs" in outs:
        ref = outs["slabs"]
        hd = y.shape[1] // n_heads
        if len(ref.shape) == 4:
            ref[:, 1:] = jnp.zeros((ref.shape[0], ref.shape[1] - 1) + ref.shape[2:], F32)
            for h in range(n_heads):
                ref[:, 0, h, :] = y[:, h * hd:(h + 1) * hd]
        else:
            for h in range(n_heads):
                ref[:, h, :] = y[:, h * hd:(h + 1) * hd]
    if "rows" in outs:
        outs["rows"][...] = y.astype(BF16)
    if "cols" in outs:
        outs["cols"][...] = y.T.astype(BF16)
    if "mean" in outs:
        outs["mean"][0] = jnp.mean(y, axis=0, keepdims=True)


def _proj(x, w, tables, slabs_in, *, layer, part, tile0, n_tiles, n_layers, rope, outs_wanted, n_heads, rot, name):
    d = x.shape[1]
    rows = n_tiles * TM
    hd = d // n_heads
    kern = functools.partial(_proj_kernel, rope=rope, outs_wanted=outs_wanted, n_heads=n_heads, rot=rot)
    in_specs = [pl.BlockSpec((TM, d), lambda i: (tile0 + i, 0)), _layer_spec((d, d), (layer, 0, part))]
    args = [x, w]
    if rope:
        in_specs += [pl.BlockSpec((TM, LANES), lambda i: (tile0 + i, 0))] * 3
        args += list(tables)
    aliases = {}
    if slabs_in is not None:
        aliases = {len(args): outs_wanted.index("slabs")}
        in_specs.append(pl.BlockSpec(memory_space=pl.ANY))
        args.append(slabs_in)
        kern = functools.partial(_drop_ref, kern, len(args) - 1)
        slab_spec = pl.BlockSpec((TM, None, n_heads, hd), lambda i: (i, layer, 0, 0))
    else:
        slab_spec = pl.BlockSpec((TM, n_layers, n_heads, hd), lambda i: (i, 0, 0, 0))
    specs = {
        "slabs": (slab_spec, jax.ShapeDtypeStruct((rows, n_layers, n_heads, hd), F32)),
        "rows": (pl.BlockSpec((TM, d), lambda i: (i, 0)), jax.ShapeDtypeStruct((rows, d), BF16)),
        "cols": (pl.BlockSpec((d, TM), lambda i: (0, i)), jax.ShapeDtypeStruct((d, rows), BF16)),
        "mean": (pl.BlockSpec((1, 1, d), lambda i: (i, 0, 0)), jax.ShapeDtypeStruct((n_tiles, 1, d), F32)),
    }
    return pl.pallas_call(
        kern, grid=(n_tiles,), in_specs=in_specs,
        out_specs=[specs[o][0] for o in outs_wanted], out_shape=[specs[o][1] for o in outs_wanted],
        input_output_aliases=aliases,
        compiler_params=_cparams(1), name=name,
    )(*args)


def _drop_ref(kern, at, *refs):
    return kern(*refs[:at], *refs[at + 1:])


def _top_blocks(gate, idx, topk, axis):
    big = float(gate.shape[axis])
    ids = []
    for _ in range(topk):
        m = jnp.max(gate, axis=axis, keepdims=True)
        sel = jnp.min(jnp.where(gate == m, idx, big), axis=axis, keepdims=True)
        sel = jnp.where(m > NEG_INF, sel, -1.0)
        ids.append(sel)
        gate = jnp.where(idx == sel, NEG_INF, gate)
    return ids


def _online_softmax(s, keep, m, l, c2):
    ms, ls, scales, ps = [], [], [], []
    for q0 in range(0, s.shape[1], LANES):
        qs = slice(q0, q0 + LANES)
        sh = jnp.where(keep[:, qs], s[:, qs], NEG_INF)
        m_new = jnp.maximum(m[:, qs], jnp.max(sh, axis=0, keepdims=True))
        a = jnp.exp2((m[:, qs] - m_new) * c2)
        p = jnp.exp2((sh - m_new) * c2)
        ms.append(m_new)
        ls.append(a * l[:, qs] + jnp.sum(p, axis=0, keepdims=True))
        scales.append(a)
        ps.append(p.astype(BF16))
    cat = lambda xs: jnp.concatenate(xs, axis=1)
    return cat(ms), cat(ls), cat(scales), cat(ps)


def _moba_prompt_kernel(qt_ref, k_ref, vt_ref, km_ref, o_ref, s_buf, p_buf, acc_buf, *, blk, topk, scale, hg, hd):
    n_blk = k_ref.shape[0] // blk
    kmb = km_ref[...].astype(BF16)
    key_i = lax.broadcasted_iota(jnp.int32, (blk, blk), 0)
    qry_i = lax.broadcasted_iota(jnp.int32, (blk, blk), 1)
    causal = key_i <= qry_i
    blk_i = lax.broadcasted_iota(jnp.int32, (n_blk, blk), 0).astype(F32)
    c2 = scale * LOG2_E

    def q_tile(qi, carry):
        q0 = _aligned(qi * blk, blk)
        qf = jnp.asarray(qi, F32)
        ids, init = [], []
        for h in range(hg):
            hs = slice(h * hd, (h + 1) * hd)
            qt = qt_ref[hs, pl.ds(q0, blk)]
            s_buf[h] = jnp.dot(k_ref[pl.ds(0, blk), hs], qt, preferred_element_type=F32)
            gate = jnp.dot(kmb[:, hs], qt, preferred_element_type=F32)
            gate = jnp.where(blk_i < qf, gate, NEG_INF)
            ids.append(_top_blocks(gate, blk_i, topk, 0))
            s = jnp.dot(k_ref[pl.ds(q0, blk), hs], qt, preferred_element_type=F32)
            m, l, _, p = _online_softmax(s, causal, jnp.full((1, blk), NEG_INF, F32), jnp.zeros((1, blk), F32), c2)
            p_buf[h] = p
            acc_buf[h] = jnp.zeros((hd, blk), F32)
            init.append((m, l, jnp.zeros((1, blk), F32)))

        def kv_step(n, c):
            prev0, stats = c
            prev0 = _aligned(prev0, blk)
            nf = jnp.asarray(n, F32)
            nxt0 = _aligned(jnp.minimum(n + 1, qi - 1) * blk, blk)
            out = []
            for h in range(hg):
                hs = slice(h * hd, (h + 1) * hd)
                m, l, a_prev = stats[h]
                s = s_buf[h]
                s_buf[h] = jnp.dot(k_ref[pl.ds(nxt0, blk), hs], qt_ref[hs, pl.ds(q0, blk)], preferred_element_type=F32)
                pv = jnp.dot(vt_ref[hs, pl.ds(prev0, blk)], p_buf[h], preferred_element_type=F32)
                acc_buf[h] = a_prev * acc_buf[h] + pv
                picked = ids[h][0] == nf
                for t in range(1, topk):
                    picked = picked | (ids[h][t] == nf)
                m, l, a, p = _online_softmax(s, picked, m, l, c2)
                p_buf[h] = p
                out.append((m, l, a))
            return _aligned(n * blk, blk), tuple(out)

        prev0, fin = lax.fori_loop(0, qi, kv_step, (jnp.asarray(q0, jnp.int32), tuple(init)))
        prev0 = _aligned(prev0, blk)
        for h in range(hg):
            hs = slice(h * hd, (h + 1) * hd)
            m, l, a_prev = fin[h]
            acc = a_prev * acc_buf[h] + jnp.dot(vt_ref[hs, pl.ds(prev0, blk)], p_buf[h], preferred_element_type=F32)
            o_ref[pl.ds(q0, blk), hs] = (acc / l).T.astype(o_ref.dtype)
        return carry

    lax.fori_loop(0, n_blk, q_tile, 0)


def _moba_prompt(qt, k, vt, km2, *, bp, tp, n_heads, hd):
    d = k.shape[1]
    hg = min(MOBA_HEADS_PER_STEP, n_heads)
    n_blk = tp // MOBA_BLOCK
    kern = functools.partial(_moba_prompt_kernel, blk=MOBA_BLOCK, topk=MOBA_TOPK, scale=hd ** -0.5, hg=hg, hd=hd)
    row_spec = pl.BlockSpec((tp, hg * hd), lambda b, g: (b, g))
    col_spec = pl.BlockSpec((hg * hd, tp), lambda b, g: (g, b))
    return pl.pallas_call(
        kern, grid=(bp, n_heads // hg),
        in_specs=[col_spec, row_spec, col_spec, pl.BlockSpec((n_blk, hg * hd), lambda b, g: (b, g))],
        out_specs=row_spec,
        out_shape=jax.ShapeDtypeStruct((bp * tp, d), BF16),
        scratch_shapes=[pltpu.VMEM((hg, MOBA_BLOCK, MOBA_BLOCK), F32), pltpu.VMEM((hg, MOBA_BLOCK, MOBA_BLOCK), BF16),
                        pltpu.VMEM((hg, hd, MOBA_BLOCK), F32)],
        compiler_params=_cparams(2), name="moba_prompt",
    )(qt, k, vt, km2)


def _cache_mean_kernel(pt_ref, *refs, ppb, bps, inv):
    o_ref = refs[ppb * bps]
    for blk in range(bps):
        acc = jnp.sum(refs[blk * ppb][...], axis=0)
        for t in range(1, ppb):
            acc = acc + jnp.sum(refs[blk * ppb + t][...], axis=0)
        o_ref[0, blk] = acc * inv


def _cache_means(cache_k, pt_flat, *, bs, n_pages):
    _, page, nbl, n_heads, hd = cache_k.shape
    ppb = MOBA_BLOCK // page
    n_cblk = n_pages // ppb
    bps = 2 if n_cblk % 2 == 0 else 1
    kern = functools.partial(_cache_mean_kernel, ppb=ppb, bps=bps, inv=1.0 / MOBA_BLOCK)

    def page_spec(t):
        return pl.BlockSpec((None, page, nbl, n_heads, hd),
                            lambda b, n, pt: (pt[b * n_pages + n * ppb * bps + t], 0, 0, 0, 0))

    return pl.pallas_call(
        kern,
        grid_spec=pltpu.PrefetchScalarGridSpec(
            num_scalar_prefetch=1, grid=(bs, n_cblk // bps),
            in_specs=[page_spec(t) for t in range(ppb * bps)],
            out_specs=pl.BlockSpec((1, bps, nbl, n_heads, hd), lambda b, n, pt: (b, n, 0, 0, 0)),
        ),
        out_shape=jax.ShapeDtypeStruct((bs, n_cblk, nbl, n_heads, hd), F32),
        compiler_params=_cparams(2), name="cache_block_means",
    )(pt_flat, *([cache_k] * (ppb * bps)))


def _sample_gate_kernel(q_ref, km_ref, o_ref, *, topk):
    n_heads = q_ref.shape[1]
    n_cblk = km_ref.shape[2]
    nt_dims = (((1,), (1,)), ((), ()))
    lane = lax.broadcasted_iota(jnp.int32, (Q_PAD, n_cblk), 1).astype(F32)
    out_lane = lax.broadcasted_iota(jnp.int32, (Q_PAD, LANES), 1)
    for h in range(n_heads):
        gate = lax.dot_general(q_ref[0, h], km_ref[0, h].astype(BF16), nt_dims, preferred_element_type=F32)
        ids = _top_blocks(gate, lane, topk, 1)
        out = jnp.zeros((Q_PAD, LANES), F32)
        for t in range(topk):
            out = jnp.where(out_lane == t, ids[t], out)
        o_ref[0, h] = out.astype(jnp.int32)


def _sample_gate(q_s, km_s):
    bs, n_heads, _, hd = q_s.shape
    n_cblk = km_s.shape[2]
    kern = functools.partial(_sample_gate_kernel, topk=MOBA_TOPK)
    return pl.pallas_call(
        kern, grid=(bs,),
        in_specs=[pl.BlockSpec((1, n_heads, Q_PAD, hd), lambda b: (b, 0, 0, 0)),
                  pl.BlockSpec((1, n_heads, n_cblk, hd), lambda b: (b, 0, 0, 0))],
        out_specs=pl.BlockSpec((1, n_heads, Q_PAD, LANES), lambda b: (b, 0, 0, 0)),
        out_shape=jax.ShapeDtypeStruct((bs, n_heads, Q_PAD, LANES), jnp.int32),
        compiler_params=_cparams(1), name="sample_gate",
    )(q_s, km_s)


def _sample_attn_kernel(pt_ref, ids_ref, ck_hbm, cv_hbm, q_ref, kn_ref, vn_ref, o_ref, kbuf, vbuf, sem,
                        *, layer, ts, topk, ppb, n_pages, scale):
    b, h = pl.program_id(0), pl.program_id(1)
    n_heads = pl.num_programs(1)
    step = b * n_heads + h
    n_steps = pl.num_programs(0) * n_heads
    page = kbuf.shape[2]

    def copies(bb, hh, slot):
        out = []
        for qi in range(ts):
            for t in range(topk):
                blk = ids_ref[((bb * n_heads + hh) * ts + qi) * topk + t]
                for half in range(ppb):
                    pg = pt_ref[bb * n_pages + blk * ppb + half]
                    j = (qi * topk + t) * ppb + half
                    out.append(pltpu.make_async_copy(ck_hbm.at[pg, :, layer, hh, :], kbuf.at[slot, j], sem.at[0, slot]))
                    out.append(pltpu.make_async_copy(cv_hbm.at[pg, :, layer, hh, :], vbuf.at[slot, j], sem.at[1, slot]))
        return out

    @pl.when(step == 0)
    def _():
        for c in copies(b, h, 0):
            c.start()

    @pl.when(step + 1 < n_steps)
    def _():
        nxt = step + 1
        for c in copies(nxt // n_heads, nxt % n_heads, nxt % 2):
            c.start()

    slot = step % 2
    for c in copies(b, h, slot):
        c.wait()

    nt_dims = (((1,), (1,)), ((), ()))
    q = q_ref[0, 0]
    rows = lax.broadcasted_iota(jnp.int32, (Q_PAD, Q_PAD), 0)
    cols = lax.broadcasted_iota(jnp.int32, (Q_PAD, Q_PAD), 1)
    s_new = lax.dot_general(q, kn_ref[0, 0], nt_dims, preferred_element_type=F32) * scale
    s_new = jnp.where((cols <= rows) & (cols < ts), s_new, NEG_INF)
    out_rows = lax.broadcasted_iota(jnp.int32, (Q_PAD, q.shape[1]), 0)
    out = jnp.zeros((Q_PAD, q.shape[1]), F32)
    per_q = topk * ppb
    for qi in range(ts):
        kq = kbuf[slot, qi * per_q:(qi + 1) * per_q].reshape(per_q * page, -1).astype(BF16)
        vq = vbuf[slot, qi * per_q:(qi + 1) * per_q].reshape(per_q * page, -1).astype(BF16)
        s = lax.dot_general(q, kq, nt_dims, preferred_element_type=F32) * scale
        m = jnp.maximum(jnp.max(s, axis=1, keepdims=True), jnp.max(s_new, axis=1, keepdims=True))
        p = jnp.exp(s - m)
        pn = jnp.exp(s_new - m)
        l = jnp.sum(p, axis=1, keepdims=True) + jnp.sum(pn, axis=1, keepdims=True)
        o = jnp.dot(p.astype(BF16), vq, preferred_element_type=F32)
        o = o + jnp.dot(pn.astype(BF16), vn_ref[0, 0], preferred_element_type=F32)
        out = jnp.where(out_rows == qi, o / l, out)
    o_ref[0, 0] = out


def _sample_attn(pt_flat, ids_flat, cache_k, cache_v, q_s, kn_s, vn_s, *, layer, ts, n_pages):
    bs, n_heads, _, hd = q_s.shape
    page = cache_k.shape[1]
    ppb = MOBA_BLOCK // page
    n_sel = ts * MOBA_TOPK * ppb
    kern = functools.partial(_sample_attn_kernel, layer=layer, ts=ts, topk=MOBA_TOPK, ppb=ppb,
                             n_pages=n_pages, scale=hd ** -0.5)
    qspec = pl.BlockSpec((1, 1, Q_PAD, hd), lambda b, h, pt, ids: (b, h, 0, 0))
    return pl.pallas_call(
        kern,
        grid_spec=pltpu.PrefetchScalarGridSpec(
            num_scalar_prefetch=2, grid=(bs, n_heads),
            in_specs=[pl.BlockSpec(memory_space=pl.ANY), pl.BlockSpec(memory_space=pl.ANY), qspec, qspec, qspec],
            out_specs=qspec,
            scratch_shapes=[pltpu.VMEM((2, n_sel, page, hd), F32), pltpu.VMEM((2, n_sel, page, hd), F32),
                            pltpu.SemaphoreType.DMA((2, 2))],
        ),
        out_shape=jax.ShapeDtypeStruct((bs, n_heads, Q_PAD, hd), F32),
        compiler_params=_cparams(2), name="sample_attn",
    )(pt_flat, ids_flat, cache_k, cache_v, q_s, kn_s, vn_s)


def _oproj_kernel(a_ref, tail_ref, x_ref, wo_ref, g_ref, b_ref, wr_ref, br_ref, h_ref, hg_ref, route_ref,
                  *, alpha, n_grp, epg):
    is_tail = pl.program_id(0) == pl.num_programs(0) - 1
    a = jnp.where(is_tail, tail_ref[...], a_ref[...])
    mix = jnp.dot(a, wo_ref[...], preferred_element_type=F32)
    _post_mix(x_ref[...], mix, g_ref, b_ref, wr_ref, br_ref, h_ref, hg_ref, route_ref, alpha, n_grp, epg)


def _oproj_layer(attn, tail, x, w_o, g0, b0, wr, br, *, layer, alpha, n_grp, epg):
    nt, d = x.shape
    s_n = d // LANES
    last_prompt = attn.shape[0] // TM - 1
    kern = functools.partial(_oproj_kernel, alpha=alpha, n_grp=n_grp, epg=epg)
    return pl.pallas_call(
        kern, grid=(nt // TM,),
        in_specs=[pl.BlockSpec((TM, d), lambda i: (jnp.minimum(i, last_prompt), 0)),
                  pl.BlockSpec((TM, d), lambda i: (0, 0)),
                  pl.BlockSpec((TM, d), lambda i: (i, 0)),
                  _layer_spec((d, d), (layer, 0, 0)), _const_spec((1, d)), _const_spec((1, d)),
                  _const_spec((d, LANES)), _const_spec((1, LANES))],
        out_specs=[pl.BlockSpec((TM, d), lambda i: (i, 0)),
                   pl.BlockSpec((TM * s_n, LANES), lambda i: (i, 0)),
                   pl.BlockSpec((TM, LANES), lambda i: (i, 0))],
        out_shape=[jax.ShapeDtypeStruct((nt, d), F32),
                   jax.ShapeDtypeStruct((nt * s_n, LANES), F32),
                   jax.ShapeDtypeStruct((nt, LANES), F32)],
        compiler_params=_cparams(1), name="attn_out_proj",
    )(attn, tail, x, w_o, g0, b0, wr, br)


GATHER_UNROLL = 16
DISPATCH_MAX_TILES = 3
FINISH_CHUNKS = 8


def _start_row_gather(idx_ref, idx_base, n_rows, src_hbm, dst, dst_base, s_n, sem):
    def body(rr, c):
        for k in range(GATHER_UNROLL):
            r = rr * GATHER_UNROLL + k
            row = idx_ref[idx_base + r]
            d0 = _aligned((dst_base + r) * s_n, s_n)
            pltpu.make_async_copy(src_hbm.at[row], dst.at[pl.ds(d0, s_n)], sem).start()
        return c
    lax.fori_loop(0, n_rows // GATHER_UNROLL, body, 0)


def _start_row_scatter(idx_ref, idx_base, n_rows, src, dst_hbm, s_n, sem):
    def body(rr, c):
        for k in range(GATHER_UNROLL):
            r = rr * GATHER_UNROLL + k
            s0 = _aligned(r * s_n, s_n)
            d0 = _aligned(idx_ref[idx_base + r] * s_n, s_n)
            pltpu.make_async_copy(src.at[pl.ds(s0, s_n)], dst_hbm.at[pl.ds(d0, s_n)], sem).start()
        return c
    lax.fori_loop(0, n_rows // GATHER_UNROLL, body, 0)


def _moe_plan(route, n_exp):
    nt = route.shape[0]
    e_flat = route[:, :2].astype(jnp.int32).T.reshape(-1)
    ar = jnp.arange(n_exp, dtype=jnp.int32)
    onehot = (e_flat[:, None] == ar[None, :]).astype(jnp.int32)
    csum = jnp.cumsum(onehot, axis=0)
    counts = csum[-1]
    padded = ((counts + TM - 1) // TM) * TM
    pend = jnp.cumsum(padded)
    pstart = pend - padded
    pos = jnp.sum(onehot * (csum - 1 + pstart[None, :]), axis=1)
    n_tiles = (2 * nt + n_exp * TM) // TM
    tile_start = jnp.arange(n_tiles, dtype=jnp.int32) * TM
    tile_e = jnp.sum((tile_start[:, None] >= pend[None, :]).astype(jnp.int32), axis=1)
    tile_e = jnp.minimum(tile_e, jnp.max(jnp.where(counts > 0, ar, 0)))
    n_used = (pend[-1] // TM).reshape(1)
    i32 = lambda a: a.astype(jnp.int32)
    return i32(pos), i32(tile_e), i32(n_used), i32(pend), i32(padded - counts)


def _dispatch_kernel(pos_ref, pend_ref, npad_ref, hg_ref, xs_hbm, zbuf, sem, zsem, *, nt, n_exp, s_n):
    i = pl.program_id(0)
    n_rows = xs_hbm.shape[0] // s_n
    rows = hg_ref.shape[0] // s_n

    def zero_rows(first, count):
        def body(r, c):
            d0 = _aligned((first + r) * s_n, s_n)
            pltpu.make_async_copy(zbuf, xs_hbm.at[pl.ds(d0, s_n)], zsem).start()
            return c
        lax.fori_loop(0, count, body, 0)

    @pl.when(i == 0)
    def _():
        zbuf[...] = jnp.zeros_like(zbuf)
        for e in range(n_exp):
            zero_rows(pend_ref[e] - npad_ref[e], npad_ref[e])
        zero_rows(pend_ref[n_exp - 1], n_rows - pend_ref[n_exp - 1])

    for k in range(2):
        _start_row_scatter(pos_ref, k * nt + i * rows, rows, hg_ref, xs_hbm, s_n, sem)
    for k in range(2):
        pltpu.make_async_copy(hg_ref, xs_hbm.at[pl.ds(0, rows * s_n)], sem).wait()

    @pl.when(i == pl.num_programs(0) - 1)
    def _():
        n_zero = n_exp * TM * s_n
        pltpu.make_async_copy(xs_hbm.at[pl.ds(0, n_zero)], xs_hbm.at[pl.ds(0, n_zero)], zsem).wait()


def _dispatch(hg, pos, pend, npad, n_rows, s_n):
    nt = hg.shape[0] // s_n
    n_exp = pend.shape[0]
    kern = functools.partial(_dispatch_kernel, nt=nt, n_exp=n_exp, s_n=s_n)
    tiles = max(k for k in range(1, DISPATCH_MAX_TILES + 1) if (nt // TM) % k == 0)
    rows = tiles * TM
    return pl.pallas_call(
        kern,
        grid_spec=pltpu.PrefetchScalarGridSpec(
            num_scalar_prefetch=3, grid=(nt // rows,),
            in_specs=[pl.BlockSpec((rows * s_n, LANES), lambda i, pos, pend, npad: (i, 0))],
            out_specs=pl.BlockSpec(memory_space=pl.ANY),
            scratch_shapes=[pltpu.VMEM((s_n, LANES), F32), pltpu.SemaphoreType.DMA(()), pltpu.SemaphoreType.DMA(())],
        ),
        out_shape=jax.ShapeDtypeStruct((n_rows * s_n, LANES), F32),
        compiler_params=_cparams(1), name="moe_dispatch",
    )(pos, pend, npad, hg)


def _moe_kernel(te_ref, nu_ref, x_ref, wg_ref, wu_ref, wd_ref, y_ref, wgb, wub, wdb):
    i = pl.program_id(0)
    s_n = wg_ref.shape[1] // LANES
    rows = x_ref.shape[0] // s_n

    @pl.when(i < nu_ref[0])
    def _():
        @pl.when((i == 0) | (te_ref[i] != te_ref[jnp.maximum(i - 1, 0)]))
        def _():
            wgb[...] = wg_ref[0].astype(BF16)
            wub[...] = wu_ref[0].astype(BF16)
            wdb[...] = wd_ref[0].astype(BF16)

        x = _from_slabs(x_ref, 0, rows, s_n).astype(BF16)
        a = jnp.dot(x, wgb[...], preferred_element_type=F32)
        u = jnp.dot(x, wub[...], preferred_element_type=F32)
        hid = (a * jax.nn.sigmoid(a)) * u
        _to_slabs(y_ref, jnp.dot(hid.astype(BF16), wdb[...], preferred_element_type=F32))

    @pl.when(i >= nu_ref[0])
    def _():
        y_ref[...] = jnp.zeros_like(y_ref)


def _moe(xs2, tile_e, n_used, w_gate, w_up, w_down, layer):
    n_tiles = tile_e.shape[0]
    _, _, d, f = w_gate.shape
    s_n = d // LANES
    tile = lambda i, te, nu: (jnp.minimum(i, nu[0] - 1), 0)
    wspec = lambda shape: pl.BlockSpec((None, 1) + shape, lambda i, te, nu: (layer, te[i], 0, 0))
    return pl.pallas_call(
        _moe_kernel,
        grid_spec=pltpu.PrefetchScalarGridSpec(
            num_scalar_prefetch=2, grid=(n_tiles,),
            in_specs=[pl.BlockSpec((TM * s_n, LANES), tile), wspec((d, f)), wspec((d, f)), wspec((f, d))],
            out_specs=pl.BlockSpec((TM * s_n, LANES), lambda i, te, nu: (i, 0)),
            scratch_shapes=[pltpu.VMEM((d, f), BF16), pltpu.VMEM((d, f), BF16), pltpu.VMEM((f, d), BF16)],
        ),
        out_shape=jax.ShapeDtypeStruct((n_tiles * TM * s_n, LANES), F32),
        compiler_params=_cparams(1), name="moe_experts",
    )(tile_e, n_used, xs2, w_gate, w_up, w_down)


def _finish_kernel(pos_ref, h_ref, r_ref, ys_hbm, p_ref, wpg_ref, bpg_ref, wpp_ref, g1_ref, b1_ref, g2_ref, b2_ref,
                   o_ref, ybuf, sem, *, alpha, nt):
    i = pl.program_id(0)
    n = pl.num_programs(0)
    s_n = ys_hbm.shape[1]
    rows = h_ref.shape[0]

    @pl.when(i == 0)
    def _():
        for k in range(2):
            _start_row_gather(pos_ref, k * nt, rows, ys_hbm, ybuf.at[0], k * rows, s_n, sem.at[0])

    slot = i % 2
    pltpu.make_async_copy(ybuf.at[slot], ybuf.at[slot], sem.at[slot]).wait()
    y2 = ybuf.at[slot]
    r = r_ref[...]
    moe = r[:, 2:3] * _from_slabs(y2, 0, rows, s_n) + r[:, 3:4] * _from_slabs(y2, rows * s_n, rows, s_n)
    h2 = _ln(alpha * h_ref[...] + moe, g1_ref[...], b1_ref[...])
    h2b = h2.astype(BF16)
    pb = p_ref[...].astype(BF16)

    nxt = jnp.minimum(i + 1, n - 1)
    other = ybuf.at[1 - slot]
    per = 2 * rows // FINISH_CHUNKS
    cw = wpg_ref.shape[1] // FINISH_CHUNKS
    zs = []
    for c in range(FINISH_CHUNKS):
        for q in range(c * per, (c + 1) * per):
            k, rr = divmod(q, rows)
            row = pos_ref[k * nt + nxt * rows + rr]
            pltpu.make_async_copy(ys_hbm.at[row], other.at[pl.ds(q * s_n, s_n)], sem.at[1 - slot]).start()
        cs = slice(c * cw, (c + 1) * cw)
        zg = jnp.dot(h2b, wpg_ref[:, cs], preferred_element_type=F32) + bpg_ref[:, cs]
        zp = jnp.dot(pb, wpp_ref[:, cs], preferred_element_type=F32)
        zs.append(jax.nn.sigmoid(zg) * zp)
    z = jnp.concatenate(zs, axis=1)
    o_ref[...] = h2 + _ln(z, g2_ref[...], b2_ref[...])

    @pl.when(i == n - 1)
    def _():
        pltpu.make_async_copy(other, other, sem.at[1 - slot]).wait()


def _finish(pos, h, route, ys3, p, w_pg, b_pg, w_pp, g1, b1, g2, b2, *, layer, alpha):
    nt, d = h.shape
    dp = p.shape[1]
    s_n = d // LANES
    kern = functools.partial(_finish_kernel, alpha=alpha, nt=nt)
    tile = lambda i, pos: (i, 0)
    const = _const_spec
    return pl.pallas_call(
        kern,
        grid_spec=pltpu.PrefetchScalarGridSpec(
            num_scalar_prefetch=1, grid=(nt // TM,),
            in_specs=[pl.BlockSpec((TM, d), tile), pl.BlockSpec((TM, LANES), tile), pl.BlockSpec(memory_space=pl.ANY),
                      pl.BlockSpec((TM, dp), tile), _layer_spec((d, d), (layer, 0, 0)), const((1, d)),
                      _layer_spec((dp, d), (layer, 0, 0)),
                      const((1, d)), const((1, d)), const((1, d)), const((1, d))],
            out_specs=pl.BlockSpec((TM, d), tile),
            scratch_shapes=[pltpu.VMEM((2, 2 * TM * s_n, LANES), F32), pltpu.SemaphoreType.DMA((2,))],
        ),
        out_shape=jax.ShapeDtypeStruct((nt, d), F32),
        compiler_params=_cparams(1), name="layer_finish",
    )(pos, h, route, ys3, p, w_pg, b_pg, w_pp, g1, b1, g2, b2)


def _rope_tables(positions, hd, rot):
    half = rot // 2
    inv = ROPE_THETA ** (-jnp.arange(half, dtype=F32) / half)
    ang = positions.astype(F32)[:, None] * inv[None, :]
    cos, sin = jnp.cos(ang), jnp.sin(ang)
    n = positions.shape[0]
    ones = jnp.ones((n, hd - rot), F32)
    zeros_h = jnp.zeros((n, half), F32)
    zeros_r = jnp.zeros((n, hd - rot), F32)
    cos_t = jnp.concatenate([cos, cos, ones], axis=1)
    sa = jnp.concatenate([-sin, zeros_h, zeros_r], axis=1)
    sb = jnp.concatenate([zeros_h, sin, zeros_r], axis=1)
    return cos_t, sa, sb


def kernel(x_prompt, x_sample, cache_k, cache_v, page_table, p_prompt, p_sample, a_w_in, a_ln_g, a_ln_b, a_w_s, a_b_s, a_w_out, b_w_qkv, b_w_o, ln_g, ln_b, moe_w_rc, moe_b_rc, moe_w_rf, moe_b_rf, moe_w_gate, moe_w_up, moe_w_down, ple_w_proj, ple_w_gate, ple_b_gate):
    bp, tp, d = x_prompt.shape
    bs, ts, _ = x_sample.shape
    depth = ln_g.shape[0]
    _, page, nbl, n_heads, hd = cache_k.shape
    n_pages = page_table.shape[1]
    past_len = n_pages * page
    n_groups, chunk = a_w_s.shape[1], a_w_s.shape[2]
    da = a_w_out.shape[1]
    n_grp, epg = moe_w_rf.shape[2], moe_w_rf.shape[3]
    n_exp = n_grp * epg
    rot = hd // 4
    alpha = (2.0 * depth) ** 0.25
    np_rows, ns_rows = bp * tp, bs * ts
    nt = np_rows + TM
    s_n = d // LANES

    assert tp % MOBA_BLOCK == 0 and past_len % MOBA_BLOCK == 0 and MOBA_BLOCK % page == 0
    assert ts <= min(chunk, Q_PAD) and chunk % ts == 0 and ns_rows <= chunk and TM % chunk == 0
    assert n_heads * hd == d and hd == LANES and n_grp + n_exp <= LANES and n_pages // (MOBA_BLOCK // page) >= MOBA_TOPK

    def stream(prompt, sample):
        width = prompt.shape[-1]
        pad = jnp.zeros((TM - ns_rows, width), prompt.dtype)
        return jnp.concatenate([prompt.reshape(np_rows, width), sample.reshape(ns_rows, width), pad], axis=0)

    x = stream(x_prompt, x_sample)
    pt_flat = page_table.reshape(-1).astype(jnp.int32)

    positions = jnp.concatenate([jnp.tile(jnp.arange(tp, dtype=jnp.int32), bp),
                                 jnp.tile(past_len + jnp.arange(ts, dtype=jnp.int32), bs),
                                 jnp.zeros((TM - ns_rows,), jnp.int32)])
    tables = _rope_tables(positions, hd, rot)

    r = np.arange(chunk)
    tril = (r[None, :] <= r[:, None])
    same_seq = (r[None, :] // ts == r[:, None] // ts) & (r[:, None] < ns_rows) & (r[None, :] < ns_rows)
    msk2 = jnp.asarray(np.stack([tril, tril & same_seq]).astype(np.float32))

    km_cache = _cache_means(cache_k, pt_flat, bs=bs, n_pages=n_pages)
    n_sorted = 2 * nt + n_exp * TM

    w_in_b, w_out_b = a_w_in.astype(BF16), a_w_out.astype(BF16)
    w_qkv_b, w_o_b = b_w_qkv.astype(BF16), b_w_o.astype(BF16)
    w_pg_b, w_pp_b = ple_w_gate.astype(BF16), ple_w_proj.astype(BF16)

    np_tiles = np_rows // TM
    k_acc = v_acc = k_tl = v_tl = None
    chunk_rows = []
    for i in range(depth):
        j = i // 2
        wr = jnp.concatenate([moe_w_rc[i], moe_w_rf[i].reshape(d, n_exp),
                              jnp.zeros((d, LANES - n_grp - n_exp), F32)], axis=1).astype(BF16)
        br = jnp.concatenate([moe_b_rc[i], moe_b_rf[i].reshape(n_exp),
                              jnp.zeros((LANES - n_grp - n_exp,), F32)]).reshape(1, LANES)
        g0, b0 = ln_g[i, 0].reshape(1, d), ln_b[i, 0].reshape(1, d)
        if i % 2 == 0:
            ws_s = jnp.tile(a_w_s[j][:, :ts, :ts], (1, chunk // ts, chunk // ts))
            ws2 = jnp.stack([a_w_s[j], ws_s])
            bs_p = jnp.repeat(a_b_s[j].T, da // n_groups, axis=1)
            bs_s = jnp.repeat(jnp.tile(a_b_s[j][:, :ts], (1, chunk // ts)).T, da // n_groups, axis=1)
            bs2 = jnp.stack([bs_p, bs_s])
            h, hg, route, vn = _gmlp_layer(
                x, w_in_b, a_ln_g[j].reshape(1, da), a_ln_b[j].reshape(1, da),
                ws2, msk2, bs2, w_out_b, g0, b0, wr, br, layer=j, alpha=alpha, n_grp=n_grp, epg=epg)
            chunk_rows.append(vn[:ns_rows].reshape(bs, ts, da))
        else:
            proj = functools.partial(_proj, x, w_qkv_b, layer=j, n_layers=nbl, n_heads=n_heads, rot=rot)
            head = dict(tile0=0, n_tiles=np_tiles)
            tail_t = dict(tile0=np_tiles, n_tiles=1)
            (q_t,) = proj(tables, None, part=0, rope=True, outs_wanted=("cols",), name="q_proj", **head)
            (q_tl,) = proj(tables, None, part=0, rope=True, outs_wanted=("cols",), name="q_proj_tail", **tail_t)
            k_acc, k_bf, km = proj(tables, k_acc, part=1, rope=True, outs_wanted=("slabs", "rows", "mean"),
                                   name="k_proj", **head)
            k_tl, k_bf_tl = proj(tables, k_tl, part=1, rope=True, outs_wanted=("slabs", "rows"),
                                 name="k_proj_tail", **tail_t)
            v_acc, v_t = proj(None, v_acc, part=2, rope=False, outs_wanted=("slabs", "cols"), name="v_proj", **head)
            (v_tl,) = proj(None, v_tl, part=2, rope=False, outs_wanted=("slabs",), name="v_proj_tail", **tail_t)
            attn = _moba_prompt(q_t, k_bf, v_t, km.reshape(np_tiles, d), bp=bp, tp=tp, n_heads=n_heads, hd=hd)

            def heads(a):
                a = a.reshape(bs, ts, n_heads, hd).transpose(0, 2, 1, 3).astype(BF16)
                return jnp.pad(a, ((0, 0), (0, 0), (0, Q_PAD - ts), (0, 0)))

            q_s, kn_s, vn_s = heads(q_tl[:, :ns_rows].T), heads(k_bf_tl[:ns_rows]), heads(v_tl[:ns_rows, j])
            ids = _sample_gate(q_s, km_cache[:, :, j].transpose(0, 2, 1, 3))
            ids_flat = ids[:, :, :ts, :MOBA_TOPK].reshape(-1)
            o_s = _sample_attn(pt_flat, ids_flat, cache_k, cache_v, q_s, kn_s, vn_s, layer=j, ts=ts, n_pages=n_pages)
            o_s = o_s[:, :, :ts].transpose(0, 2, 1, 3).reshape(ns_rows, d).astype(BF16)
            tail = jnp.concatenate([o_s, jnp.zeros((TM - ns_rows, d), BF16)], axis=0)
            h, hg, route = _oproj_layer(attn, tail, x, w_o_b, g0, b0, wr, br,
                                        layer=j, alpha=alpha, n_grp=n_grp, epg=epg)
        pos, tile_e, n_used, pend, npad = _moe_plan(route, n_exp)
        xs = _dispatch(hg, pos, pend, npad, n_sorted, s_n)
        ys = _moe(xs, tile_e, n_used, moe_w_gate, moe_w_up, moe_w_down, i)
        p = stream(p_prompt[i], p_sample[i])
        x = _finish(pos, h, route, ys.reshape(n_sorted, s_n, LANES), p, w_pg_b,
                    ple_b_gate[i].reshape(1, d), w_pp_b,
                    ln_g[i, 1].reshape(1, d), ln_b[i, 1].reshape(1, d),
                    ln_g[i, 2].reshape(1, d), ln_b[i, 2].reshape(1, d), layer=i, alpha=alpha)

    prompt_shape, sample_shape = (bp, tp, nbl, n_heads, hd), (bs, ts, nbl, n_heads, hd)
    y_prompt = x[:np_rows].reshape(bp, tp, d)
    y_sample = x[np_rows:np_rows + ns_rows].reshape(bs, ts, d)
    chunk_v = jnp.stack(chunk_rows, axis=2)
    return (y_prompt, y_sample, k_acc.reshape(prompt_shape), v_acc.reshape(prompt_shape),
            k_tl[:ns_rows].reshape(sample_shape), v_tl[:ns_rows].reshape(sample_shape), chunk_v)
```

```python
import functools

import numpy as np
import jax
import jax.numpy as jnp
from jax import lax
from jax.experimental import pallas as pl
from jax.experimental.pallas import tpu as pltpu

F32 = jnp.float32
BF16 = jnp.bfloat16

LANES = 128
TM = 256
MOBA_BLOCK = 256
MOBA_TOPK = 3
MOBA_HEADS_PER_STEP = 4
ROPE_THETA = 500000.0
LN_EPS = 1e-5
Q_PAD = 16
VMEM_LIMIT = 56 * 1024 * 1024
NEG_INF = float("-inf")
LOG2_E = 1.4426950408889634


def _cparams(n_axes):
    return pltpu.CompilerParams(dimension_semantics=("arbitrary",) * n_axes,
                                vmem_limit_bytes=VMEM_LIMIT)


def _const_spec(shape):
    zeros = (0,) * len(shape)
    return pl.BlockSpec(shape, lambda *a: zeros, pipeline_mode=pl.Buffered(1))


def _layer_spec(shape, idx):
    return pl.BlockSpec((None,) + shape, lambda *a: idx, pipeline_mode=pl.Buffered(1))


def _aligned(x, m):
    return x if isinstance(x, int) else pl.multiple_of(x, m)


def _ln(x, g, b):
    mu = jnp.mean(x, axis=-1, keepdims=True)
    xc = x - mu
    var = jnp.mean(xc * xc, axis=-1, keepdims=True)
    return xc * lax.rsqrt(var + LN_EPS) * g + b


def _to_slabs(ref, val):
    rows, d = val.shape
    s_n = d // LANES
    for s in range(s_n):
        ref[pl.ds(s, rows, stride=s_n), :] = val[:, s * LANES:(s + 1) * LANES]


def _from_slabs(ref, base, rows, s_n):
    cols = [ref[pl.ds(base + s, rows, stride=s_n), :] for s in range(s_n)]
    return jnp.concatenate(cols, axis=1)


def _route(h, wr_ref, br_ref, n_grp, epg):
    logits = jnp.dot(h.astype(BF16), wr_ref[...], preferred_element_type=F32) + br_ref[...]
    lane = lax.broadcasted_iota(jnp.int32, logits.shape, 1).astype(F32)
    big = float(LANES)
    lc = jnp.where(lane < n_grp, logits, NEG_INF)
    mc = jnp.max(lc, axis=1, keepdims=True)
    zc = jnp.sum(jnp.exp(lc - mc), axis=1, keepdims=True)
    pg = 1.0 / zc
    g = jnp.min(jnp.where(lc == mc, lane, big), axis=1, keepdims=True)
    lo = n_grp + g * epg
    lf = jnp.where((lane >= lo) & (lane < lo + epg), logits, NEG_INF)
    m1 = jnp.max(lf, axis=1, keepdims=True)
    i1 = jnp.min(jnp.where(lf == m1, lane, big), axis=1, keepdims=True)
    lf2 = jnp.where(lane == i1, NEG_INF, lf)
    m2 = jnp.max(lf2, axis=1, keepdims=True)
    i2 = jnp.min(jnp.where(lf2 == m2, lane, big), axis=1, keepdims=True)
    e2 = jnp.exp(m2 - m1)
    w1 = pg / (1.0 + e2)
    w2 = pg * e2 / (1.0 + e2)
    out = jnp.where(lane == 0, i1 - n_grp, 0.0)
    out = jnp.where(lane == 1, i2 - n_grp, out)
    out = jnp.where(lane == 2, w1, out)
    out = jnp.where(lane == 3, w2, out)
    return out


def _post_mix(x, mix, g_ref, b_ref, wr_ref, br_ref, h_ref, hg_ref, route_ref, alpha, n_grp, epg):
    h = _ln(alpha * x + mix, g_ref[...], b_ref[...])
    h_ref[...] = h
    _to_slabs(hg_ref, h)
    route_ref[...] = _route(h, wr_ref, br_ref, n_grp, epg)


def _gmlp_kernel(x_ref, xt_ref, win_ref, ag_ref, ab_ref, ws_ref, msk_ref, bs_ref, wout_ref,
                 g_ref, b_ref, wr_ref, br_ref,
                 h_ref, hg_ref, route_ref, vn_ref, gated_ref,
                 *, n_groups, chunk, alpha, n_grp, epg):
    is_tail = pl.program_id(0) == pl.num_programs(0) - 1
    x = jnp.where(is_tail, xt_ref[...], x_ref[...])
    z = jnp.dot(x.astype(BF16), win_ref[...], preferred_element_type=F32)
    z = jax.nn.gelu(z)
    da = z.shape[1] // 2
    u = z[:, :da]
    vn = _ln(z[:, da:], ag_ref[...], ab_ref[...])

    @pl.when(is_tail)
    def _():
        vn_ref[...] = vn

    vnb = vn.astype(BF16)
    gd = da // n_groups
    mask = msk_ref[0] > 0.0
    for g in range(n_groups):
        wsg = jnp.where(mask, ws_ref[0, g], 0.0).astype(BF16)
        for c in range(x.shape[0] // chunk):
            r0 = c * chunk
            s = jnp.dot(wsg, vnb[r0:r0 + chunk, g * gd:(g + 1) * gd], preferred_element_type=F32)
            s = s + bs_ref[0][:, g * gd:(g + 1) * gd]
            gated_ref[r0:r0 + chunk, g * gd:(g + 1) * gd] = (u[r0:r0 + chunk, g * gd:(g + 1) * gd] * s).astype(BF16)
    mix = jnp.dot(gated_ref[...], wout_ref[...], preferred_element_type=F32)
    _post_mix(x, mix, g_ref, b_ref, wr_ref, br_ref, h_ref, hg_ref, route_ref, alpha, n_grp, epg)


def _gmlp_layer(x, x_tail, w_in, a_g, a_b, ws2, msk2, bs2, w_out, g0, b0, wr, br, *, n_tiles, layer, alpha, n_grp, epg):
    d = x.shape[1]
    nt = n_tiles * TM
    da = w_out.shape[1]
    n_groups, chunk = ws2.shape[1], ws2.shape[2]
    last = n_tiles - 1
    s_n = d // LANES
    kern = functools.partial(_gmlp_kernel, n_groups=n_groups, chunk=chunk, alpha=alpha, n_grp=n_grp, epg=epg)
    sel = lambda i: (i // last, 0, 0, 0) if last > 0 else (0, 0, 0, 0)
    sel3 = lambda i: (i // last, 0, 0) if last > 0 else (0, 0, 0)
    return pl.pallas_call(
        kern,
        grid=(n_tiles,),
        in_specs=[
            pl.BlockSpec((TM, d), lambda i: (jnp.minimum(i, last - 1), 0)),
            pl.BlockSpec((TM, d), lambda i: (0, 0)),
            _layer_spec((d, 2 * da), (layer, 0, 0)),
            _const_spec((1, da)), _const_spec((1, da)),
            pl.BlockSpec((1, n_groups, chunk, chunk), sel),
            pl.BlockSpec((1, chunk, chunk), sel3),
            pl.BlockSpec((1, chunk, da), sel3),
            _layer_spec((da, d), (layer, 0, 0)),
            _const_spec((1, d)), _const_spec((1, d)),
            _const_spec((d, LANES)), _const_spec((1, LANES)),
        ],
        out_specs=[
            pl.BlockSpec((TM, d), lambda i: (i, 0)),
            pl.BlockSpec((TM * s_n, LANES), lambda i: (i, 0)),
            pl.BlockSpec((TM, LANES), lambda i: (i, 0)),
            pl.BlockSpec((TM, da), lambda i: (0, 0)),
        ],
        out_shape=[
            jax.ShapeDtypeStruct((nt, d), F32),
            jax.ShapeDtypeStruct((nt * s_n, LANES), F32),
            jax.ShapeDtypeStruct((nt, LANES), F32),
            jax.ShapeDtypeStruct((TM, da), F32),
        ],
        scratch_shapes=[pltpu.VMEM((TM, da), BF16)],
        compiler_params=_cparams(1),
        name="gmlp_mixer",
    )(x, x_tail, w_in, a_g, a_b, ws2, msk2, bs2, w_out, g0, b0, wr, br)


def _proj_kernel(*refs, rope, outs_wanted, n_heads, rot):
    x_ref, w_ref = refs[0], refs[1]
    pos = 2
    if rope:
        cos_ref, sa_ref, sb_ref = refs[2:5]
        pos = 5
    outs = dict(zip(outs_wanted, refs[pos:]))
    y = jnp.dot(x_ref[...].astype(BF16), w_ref[...], preferred_element_type=F32)
    if rope:
        hd = y.shape[1] // n_heads
        half = rot // 2
        cos, sa, sb = cos_ref[...], sa_ref[...], sb_ref[...]
        parts = []
        for h in range(n_heads):
            yh = y[:, h * hd:(h + 1) * hd]
            parts.append(yh * cos + pltpu.roll(yh, hd - half, 1) * sa + pltpu.roll(yh, half, 1) * sb)
        y = jnp.concatenate(parts, axis=1)
    if "slabs" in outs:
        ref = outs["slabs"]
        hd = y.shape[1] // n_heads
        if len(ref.shape) == 4:
            ref[:, 1:] = jnp.zeros((ref.shape[0], ref.shape[1] - 1) + ref.shape[2:], F32)
            for h in range(n_heads):
                ref[:, 0, h, :] = y[:, h * hd:(h + 1) * hd]
        else:
            for h in range(n_heads):
                ref[:, h, :] = y[:, h * hd:(h + 1) * hd]
    if "rows" in outs:
        outs["rows"][...] = y.astype(BF16)
    if "cols" in outs:
        outs["cols"][...] = y.T.astype(BF16)
    if "mean" in outs:
        outs["mean"][0] = jnp.mean(y, axis=0, keepdims=True)


def _proj(x, w, tables, slabs_in, *, layer, part, tile0, n_tiles, n_layers, rope, outs_wanted, n_heads, rot, name):
    d = x.shape[1]
    rows = n_tiles * TM
    hd = d // n_heads
    kern = functools.partial(_proj_kernel, rope=rope, outs_wanted=outs_wanted, n_heads=n_heads, rot=rot)
    in_specs = [pl.BlockSpec((TM, d), lambda i: (tile0 + i, 0)), _layer_spec((d, d), (layer, 0, part))]
    args = [x, w]
    if rope:
        in_specs += [pl.BlockSpec((TM, LANES), lambda i: (tile0 + i, 0))] * 3
        args += list(tables)
    aliases = {}
    if slabs_in is not None:
        aliases = {len(args): outs_wanted.index("slabs")}
        in_specs.append(pl.BlockSpec(memory_space=pl.ANY))
        args.append(slabs_in)
        kern = functools.partial(_drop_ref, kern, len(args) - 1)
        slab_spec = pl.BlockSpec((TM, None, n_heads, hd), lambda i: (i, layer, 0, 0))
    else:
        slab_spec = pl.BlockSpec((TM, n_layers, n_heads, hd), lambda i: (i, 0, 0, 0))
    specs = {
        "slabs": (slab_spec, jax.ShapeDtypeStruct((rows, n_layers, n_heads, hd), F32)),
        "rows": (pl.BlockSpec((TM, d), lambda i: (i, 0)), jax.ShapeDtypeStruct((rows, d), BF16)),
        "cols": (pl.BlockSpec((d, TM), lambda i: (0, i)), jax.ShapeDtypeStruct((d, rows), BF16)),
        "mean": (pl.BlockSpec((1, 1, d), lambda i: (i, 0, 0)), jax.ShapeDtypeStruct((n_tiles, 1, d), F32)),
    }
    return pl.pallas_call(
        kern, grid=(n_tiles,), in_specs=in_specs,
        out_specs=[specs[o][0] for o in outs_wanted], out_shape=[specs[o][1] for o in outs_wanted],
        input_output_aliases=aliases,
        compiler_params=_cparams(1), name=name,
    )(*args)


def _drop_ref(kern, at, *refs):
    return kern(*refs[:at], *refs[at + 1:])


def _top_blocks(gate, idx, topk, axis):
    big = float(gate.shape[axis])
    ids = []
    for _ in range(topk):
        m = jnp.max(gate, axis=axis, keepdims=True)
        sel = jnp.min(jnp.where(gate == m, idx, big), axis=axis, keepdims=True)
        sel = jnp.where(m > NEG_INF, sel, -1.0)
        ids.append(sel)
        gate = jnp.where(idx == sel, NEG_INF, gate)
    return ids


def _online_softmax(s, keep, m, l, c2):
    ms, ls, scales, ps = [], [], [], []
    for q0 in range(0, s.shape[1], LANES):
        qs = slice(q0, q0 + LANES)
        sh = jnp.where(keep[:, qs], s[:, qs], NEG_INF)
        m_new = jnp.maximum(m[:, qs], jnp.max(sh, axis=0, keepdims=True))
        a = jnp.exp2((m[:, qs] - m_new) * c2)
        p = jnp.exp2((sh - m_new) * c2)
        ms.append(m_new)
        ls.append(a * l[:, qs] + jnp.sum(p, axis=0, keepdims=True))
        scales.append(a)
        ps.append(p.astype(BF16))
    cat = lambda xs: jnp.concatenate(xs, axis=1)
    return cat(ms), cat(ls), cat(scales), cat(ps)


def _moba_prompt_kernel(qt_ref, k_ref, vt_ref, km_ref, o_ref, s_buf, p_buf, acc_buf, *, blk, topk, scale, hg, hd):
    n_blk = k_ref.shape[0] // blk
    kmb = km_ref[...].astype(BF16)
    key_i = lax.broadcasted_iota(jnp.int32, (blk, blk), 0)
    qry_i = lax.broadcasted_iota(jnp.int32, (blk, blk), 1)
    causal = key_i <= qry_i
    blk_i = lax.broadcasted_iota(jnp.int32, (n_blk, blk), 0).astype(F32)
    c2 = scale * LOG2_E

    def q_tile(qi, carry):
        q0 = _aligned(qi * blk, blk)
        qf = jnp.asarray(qi, F32)
        ids, init = [], []
        for h in range(hg):
            hs = slice(h * hd, (h + 1) * hd)
            qt = qt_ref[hs, pl.ds(q0, blk)]
            s_buf[h] = jnp.dot(k_ref[pl.ds(0, blk), hs], qt, preferred_element_type=F32)
            gate = jnp.dot(kmb[:, hs], qt, preferred_element_type=F32)
            gate = jnp.where(blk_i < qf, gate, NEG_INF)
            ids.append(_top_blocks(gate, blk_i, topk, 0))
            s = jnp.dot(k_ref[pl.ds(q0, blk), hs], qt, preferred_element_type=F32)
            m, l, _, p = _online_softmax(s, causal, jnp.full((1, blk), NEG_INF, F32), jnp.zeros((1, blk), F32), c2)
            p_buf[h] = p
            acc_buf[h] = jnp.zeros((hd, blk), F32)
            init.append((m, l, jnp.zeros((1, blk), F32)))

        def kv_step(n, c):
            prev0, stats = c
            prev0 = _aligned(prev0, blk)
            nf = jnp.asarray(n, F32)
            nxt0 = _aligned(jnp.minimum(n + 1, qi - 1) * blk, blk)
            out = []
            for h in range(hg):
                hs = slice(h * hd, (h + 1) * hd)
                m, l, a_prev = stats[h]
                s = s_buf[h]
                s_buf[h] = jnp.dot(k_ref[pl.ds(nxt0, blk), hs], qt_ref[hs, pl.ds(q0, blk)], preferred_element_type=F32)
                pv = jnp.dot(vt_ref[hs, pl.ds(prev0, blk)], p_buf[h], preferred_element_type=F32)
                acc_buf[h] = a_prev * acc_buf[h] + pv
                picked = ids[h][0] == nf
                for t in range(1, topk):
                    picked = picked | (ids[h][t] == nf)
                m, l, a, p = _online_softmax(s, picked, m, l, c2)
                p_buf[h] = p
                out.append((m, l, a))
            return _aligned(n * blk, blk), tuple(out)

        prev0, fin = lax.fori_loop(0, qi, kv_step, (jnp.asarray(q0, jnp.int32), tuple(init)))
        prev0 = _aligned(prev0, blk)
        for h in range(hg):
            hs = slice(h * hd, (h + 1) * hd)
            m, l, a_prev = fin[h]
            acc = a_prev * acc_buf[h] + jnp.dot(vt_ref[hs, pl.ds(prev0, blk)], p_buf[h], preferred_element_type=F32)
            o_ref[pl.ds(q0, blk), hs] = (acc / l).T.astype(o_ref.dtype)
        return carry

    lax.fori_loop(0, n_blk, q_tile, 0)


def _moba_prompt(qt, k, vt, km2, *, bp, tp, n_heads, hd):
    d = k.shape[1]
    hg = min(MOBA_HEADS_PER_STEP, n_heads)
    n_blk = tp // MOBA_BLOCK
    kern = functools.partial(_moba_prompt_kernel, blk=MOBA_BLOCK, topk=MOBA_TOPK, scale=hd ** -0.5, hg=hg, hd=hd)
    row_spec = pl.BlockSpec((tp, hg * hd), lambda b, g: (b, g))
    col_spec = pl.BlockSpec((hg * hd, tp), lambda b, g: (g, b))
    return pl.pallas_call(
        kern, grid=(bp, n_heads // hg),
        in_specs=[col_spec, row_spec, col_spec, pl.BlockSpec((n_blk, hg * hd), lambda b, g: (b, g))],
        out_specs=row_spec,
        out_shape=jax.ShapeDtypeStruct((bp * tp, d), BF16),
        scratch_shapes=[pltpu.VMEM((hg, MOBA_BLOCK, MOBA_BLOCK), F32), pltpu.VMEM((hg, MOBA_BLOCK, MOBA_BLOCK), BF16),
                        pltpu.VMEM((hg, hd, MOBA_BLOCK), F32)],
        compiler_params=_cparams(2), name="moba_prompt",
    )(qt, k, vt, km2)


def _cache_mean_kernel(pt_ref, *refs, ppb, bps, inv):
    o_ref = refs[ppb * bps]
    for blk in range(bps):
        acc = jnp.sum(refs[blk * ppb][...], axis=0)
        for t in range(1, ppb):
            acc = acc + jnp.sum(refs[blk * ppb + t][...], axis=0)
        o_ref[0, blk] = acc * inv


def _cache_means(cache_k, pt_flat, *, bs, n_pages):
    _, page, nbl, n_heads, hd = cache_k.shape
    ppb = MOBA_BLOCK // page
    n_cblk = n_pages // ppb
    bps = 2 if n_cblk % 2 == 0 else 1
    kern = functools.partial(_cache_mean_kernel, ppb=ppb, bps=bps, inv=1.0 / MOBA_BLOCK)

    def page_spec(t):
        return pl.BlockSpec((None, page, nbl, n_heads, hd),
                            lambda b, n, pt: (pt[b * n_pages + n * ppb * bps + t], 0, 0, 0, 0))

    return pl.pallas_call(
        kern,
        grid_spec=pltpu.PrefetchScalarGridSpec(
            num_scalar_prefetch=1, grid=(bs, n_cblk // bps),
            in_specs=[page_spec(t) for t in range(ppb * bps)],
            out_specs=pl.BlockSpec((1, bps, nbl, n_heads, hd), lambda b, n, pt: (b, n, 0, 0, 0)),
        ),
        out_shape=jax.ShapeDtypeStruct((bs, n_cblk, nbl, n_heads, hd), F32),
        compiler_params=_cparams(2), name="cache_block_means",
    )(pt_flat, *([cache_k] * (ppb * bps)))


def _sample_gate_kernel(q_ref, km_ref, o_ref, *, topk):
    n_heads = q_ref.shape[1]
    n_cblk = km_ref.shape[2]
    nt_dims = (((1,), (1,)), ((), ()))
    lane = lax.broadcasted_iota(jnp.int32, (Q_PAD, n_cblk), 1).astype(F32)
    out_lane = lax.broadcasted_iota(jnp.int32, (Q_PAD, LANES), 1)
    for h in range(n_heads):
        gate = lax.dot_general(q_ref[0, h], km_ref[0, h].astype(BF16), nt_dims, preferred_element_type=F32)
        ids = _top_blocks(gate, lane, topk, 1)
        out = jnp.zeros((Q_PAD, LANES), F32)
        for t in range(topk):
            out = jnp.where(out_lane == t, ids[t], out)
        o_ref[0, h] = out.astype(jnp.int32)


def _sample_gate(q_s, km_s):
    bs, n_heads, _, hd = q_s.shape
    n_cblk = km_s.shape[2]
    kern = functools.partial(_sample_gate_kernel, topk=MOBA_TOPK)
    return pl.pallas_call(
        kern, grid=(bs,),
        in_specs=[pl.BlockSpec((1, n_heads, Q_PAD, hd), lambda b: (b, 0, 0, 0)),
                  pl.BlockSpec((1, n_heads, n_cblk, hd), lambda b: (b, 0, 0, 0))],
        out_specs=pl.BlockSpec((1, n_heads, Q_PAD, LANES), lambda b: (b, 0, 0, 0)),
        out_shape=jax.ShapeDtypeStruct((bs, n_heads, Q_PAD, LANES), jnp.int32),
        compiler_params=_cparams(1), name="sample_gate",
    )(q_s, km_s)


def _sample_attn_kernel(pt_ref, ids_ref, ck_hbm, cv_hbm, q_ref, kn_ref, vn_ref, o_ref, kbuf, vbuf, sem,
                        *, layer, ts, topk, ppb, n_pages, scale):
    b, h = pl.program_id(0), pl.program_id(1)
    n_heads = pl.num_programs(1)
    step = b * n_heads + h
    n_steps = pl.num_programs(0) * n_heads
    page = kbuf.shape[2]

    def copies(bb, hh, slot):
        out = []
        for qi in range(ts):
            for t in range(topk):
                blk = ids_ref[((bb * n_heads + hh) * ts + qi) * topk + t]
                for half in range(ppb):
                    pg = pt_ref[bb * n_pages + blk * ppb + half]
                    j = (qi * topk + t) * ppb + half
                    out.append(pltpu.make_async_copy(ck_hbm.at[pg, :, layer, hh, :], kbuf.at[slot, j], sem.at[0, slot]))
                    out.append(pltpu.make_async_copy(cv_hbm.at[pg, :, layer, hh, :], vbuf.at[slot, j], sem.at[1, slot]))
        return out

    @pl.when(step == 0)
    def _():
        for c in copies(b, h, 0):
            c.start()

    @pl.when(step + 1 < n_steps)
    def _():
        nxt = step + 1
        for c in copies(nxt // n_heads, nxt % n_heads, nxt % 2):
            c.start()

    slot = step % 2
    for c in copies(b, h, slot):
        c.wait()

    nt_dims = (((1,), (1,)), ((), ()))
    q = q_ref[0, 0]
    rows = lax.broadcasted_iota(jnp.int32, (Q_PAD, Q_PAD), 0)
    cols = lax.broadcasted_iota(jnp.int32, (Q_PAD, Q_PAD), 1)
    s_new = lax.dot_general(q, kn_ref[0, 0], nt_dims, preferred_element_type=F32) * scale
    s_new = jnp.where((cols <= rows) & (cols < ts), s_new, NEG_INF)
    out_rows = lax.broadcasted_iota(jnp.int32, (Q_PAD, q.shape[1]), 0)
    out = jnp.zeros((Q_PAD, q.shape[1]), F32)
    per_q = topk * ppb
    for qi in range(ts):
        kq = kbuf[slot, qi * per_q:(qi + 1) * per_q].reshape(per_q * page, -1).astype(BF16)
        vq = vbuf[slot, qi * per_q:(qi + 1) * per_q].reshape(per_q * page, -1).astype(BF16)
        s = lax.dot_general(q, kq, nt_dims, preferred_element_type=F32) * scale
        m = jnp.maximum(jnp.max(s, axis=1, keepdims=True), jnp.max(s_new, axis=1, keepdims=True))
        p = jnp.exp(s - m)
        pn = jnp.exp(s_new - m)
        l = jnp.sum(p, axis=1, keepdims=True) + jnp.sum(pn, axis=1, keepdims=True)
        o = jnp.dot(p.astype(BF16), vq, preferred_element_type=F32)
        o = o + jnp.dot(pn.astype(BF16), vn_ref[0, 0], preferred_element_type=F32)
        out = jnp.where(out_rows == qi, o / l, out)
    o_ref[0, 0] = out


def _sample_attn(pt_flat, ids_flat, cache_k, cache_v, q_s, kn_s, vn_s, *, layer, ts, n_pages):
    bs, n_heads, _, hd = q_s.shape
    page = cache_k.shape[1]
    ppb = MOBA_BLOCK // page
    n_sel = ts * MOBA_TOPK * ppb
    kern = functools.partial(_sample_attn_kernel, layer=layer, ts=ts, topk=MOBA_TOPK, ppb=ppb,
                             n_pages=n_pages, scale=hd ** -0.5)
    qspec = pl.BlockSpec((1, 1, Q_PAD, hd), lambda b, h, pt, ids: (b, h, 0, 0))
    return pl.pallas_call(
        kern,
        grid_spec=pltpu.PrefetchScalarGridSpec(
            num_scalar_prefetch=2, grid=(bs, n_heads),
            in_specs=[pl.BlockSpec(memory_space=pl.ANY), pl.BlockSpec(memory_space=pl.ANY), qspec, qspec, qspec],
            out_specs=qspec,
            scratch_shapes=[pltpu.VMEM((2, n_sel, page, hd), F32), pltpu.VMEM((2, n_sel, page, hd), F32),
                            pltpu.SemaphoreType.DMA((2, 2))],
        ),
        out_shape=jax.ShapeDtypeStruct((bs, n_heads, Q_PAD, hd), F32),
        compiler_params=_cparams(2), name="sample_attn",
    )(pt_flat, ids_flat, cache_k, cache_v, q_s, kn_s, vn_s)


def _oproj_kernel(a_ref, tail_ref, x_ref, wo_ref, g_ref, b_ref, wr_ref, br_ref, h_ref, hg_ref, route_ref,
                  *, alpha, n_grp, epg):
    is_tail = pl.program_id(0) == pl.num_programs(0) - 1
    a = jnp.where(is_tail, tail_ref[...], a_ref[...])
    mix = jnp.dot(a, wo_ref[...], preferred_element_type=F32)
    _post_mix(x_ref[...], mix, g_ref, b_ref, wr_ref, br_ref, h_ref, hg_ref, route_ref, alpha, n_grp, epg)


def _oproj_layer(attn, tail, x, w_o, g0, b0, wr, br, *, layer, alpha, n_grp, epg):
    nt, d = x.shape
    s_n = d // LANES
    last_prompt = attn.shape[0] // TM - 1
    kern = functools.partial(_oproj_kernel, alpha=alpha, n_grp=n_grp, epg=epg)
    return pl.pallas_call(
        kern, grid=(nt // TM,),
        in_specs=[pl.BlockSpec((TM, d), lambda i: (jnp.minimum(i, last_prompt), 0)),
                  pl.BlockSpec((TM, d), lambda i: (0, 0)),
                  pl.BlockSpec((TM, d), lambda i: (i, 0)),
                  _layer_spec((d, d), (layer, 0, 0)), _const_spec((1, d)), _const_spec((1, d)),
                  _const_spec((d, LANES)), _const_spec((1, LANES))],
        out_specs=[pl.BlockSpec((TM, d), lambda i: (i, 0)),
                   pl.BlockSpec((TM * s_n, LANES), lambda i: (i, 0)),
                   pl.BlockSpec((TM, LANES), lambda i: (i, 0))],
        out_shape=[jax.ShapeDtypeStruct((nt, d), F32),
                   jax.ShapeDtypeStruct((nt * s_n, LANES), F32),
                   jax.ShapeDtypeStruct((nt, LANES), F32)],
        compiler_params=_cparams(1), name="attn_out_proj",
    )(attn, tail, x, w_o, g0, b0, wr, br)


GATHER_UNROLL = 16
DISPATCH_MAX_TILES = 3
FINISH_CHUNKS = 8


def _start_row_gather(idx_ref, idx_base, n_rows, src_hbm, dst, dst_base, s_n, sem):
    def body(rr, c):
        for k in range(GATHER_UNROLL):
            r = rr * GATHER_UNROLL + k
            row = idx_ref[idx_base + r]
            d0 = _aligned((dst_base + r) * s_n, s_n)
            pltpu.make_async_copy(src_hbm.at[row], dst.at[pl.ds(d0, s_n)], sem).start()
        return c
    lax.fori_loop(0, n_rows // GATHER_UNROLL, body, 0)


def _start_row_scatter(idx_ref, idx_base, n_rows, src, dst_hbm, s_n, sem):
    def body(rr, c):
        for k in range(GATHER_UNROLL):
            r = rr * GATHER_UNROLL + k
            s0 = _aligned(r * s_n, s_n)
            d0 = _aligned(idx_ref[idx_base + r] * s_n, s_n)
            pltpu.make_async_copy(src.at[pl.ds(s0, s_n)], dst_hbm.at[pl.ds(d0, s_n)], sem).start()
        return c
    lax.fori_loop(0, n_rows // GATHER_UNROLL, body, 0)


def _moe_plan(route, n_exp):
    nt = route.shape[0]
    e_flat = route[:, :2].astype(jnp.int32).T.reshape(-1)
    ar = jnp.arange(n_exp, dtype=jnp.int32)
    onehot = (e_flat[:, None] == ar[None, :]).astype(jnp.int32)
    csum = jnp.cumsum(onehot, axis=0)
    counts = csum[-1]
    padded = ((counts + TM - 1) // TM) * TM
    pend = jnp.cumsum(padded)
    pstart = pend - padded
    pos = jnp.sum(onehot * (csum - 1 + pstart[None, :]), axis=1)
    n_tiles = (2 * nt + n_exp * TM) // TM
    tile_start = jnp.arange(n_tiles, dtype=jnp.int32) * TM
    tile_e = jnp.sum((tile_start[:, None] >= pend[None, :]).astype(jnp.int32), axis=1)
    tile_e = jnp.minimum(tile_e, jnp.max(jnp.where(counts > 0, ar, 0)))
    n_used = (pend[-1] // TM).reshape(1)
    i32 = lambda a: a.astype(jnp.int32)
    return i32(pos), i32(tile_e), i32(n_used), i32(pend), i32(padded - counts)


def _dispatch_kernel(pos_ref, pend_ref, npad_ref, hg_ref, xs_hbm, zbuf, sem, zsem, *, nt, n_exp, s_n):
    i = pl.program_id(0)
    n_rows = xs_hbm.shape[0] // s_n
    rows = hg_ref.shape[0] // s_n

    def zero_rows(first, count):
        def body(r, c):
            d0 = _aligned((first + r) * s_n, s_n)
            pltpu.make_async_copy(zbuf, xs_hbm.at[pl.ds(d0, s_n)], zsem).start()
            return c
        lax.fori_loop(0, count, body, 0)

    @pl.when(i == 0)
    def _():
        zbuf[...] = jnp.zeros_like(zbuf)
        for e in range(n_exp):
            zero_rows(pend_ref[e] - npad_ref[e], npad_ref[e])
        zero_rows(pend_ref[n_exp - 1], n_rows - pend_ref[n_exp - 1])

    for k in range(2):
        _start_row_scatter(pos_ref, k * nt + i * rows, rows, hg_ref, xs_hbm, s_n, sem)
    for k in range(2):
        pltpu.make_async_copy(hg_ref, xs_hbm.at[pl.ds(0, rows * s_n)], sem).wait()

    @pl.when(i == pl.num_programs(0) - 1)
    def _():
        n_zero = n_exp * TM * s_n
        pltpu.make_async_copy(xs_hbm.at[pl.ds(0, n_zero)], xs_hbm.at[pl.ds(0, n_zero)], zsem).wait()


def _dispatch(hg, pos, pend, npad, n_rows, s_n):
    nt = hg.shape[0] // s_n
    n_exp = pend.shape[0]
    kern = functools.partial(_dispatch_kernel, nt=nt, n_exp=n_exp, s_n=s_n)
    tiles = max(k for k in range(1, DISPATCH_MAX_TILES + 1) if (nt // TM) % k == 0)
    rows = tiles * TM
    return pl.pallas_call(
        kern,
        grid_spec=pltpu.PrefetchScalarGridSpec(
            num_scalar_prefetch=3, grid=(nt // rows,),
            in_specs=[pl.BlockSpec((rows * s_n, LANES), lambda i, pos, pend, npad: (i, 0))],
            out_specs=pl.BlockSpec(memory_space=pl.ANY),
            scratch_shapes=[pltpu.VMEM((s_n, LANES), F32), pltpu.SemaphoreType.DMA(()), pltpu.SemaphoreType.DMA(())],
        ),
        out_shape=jax.ShapeDtypeStruct((n_rows * s_n, LANES), F32),
        compiler_params=_cparams(1), name="moe_dispatch",
    )(pos, pend, npad, hg)


def _moe_kernel(te_ref, nu_ref, x_ref, wg_ref, wu_ref, wd_ref, y_ref, wgb, wub, wdb):
    i = pl.program_id(0)
    s_n = wg_ref.shape[1] // LANES
    rows = x_ref.shape[0] // s_n

    @pl.when(i < nu_ref[0])
    def _():
        @pl.when((i == 0) | (te_ref[i] != te_ref[jnp.maximum(i - 1, 0)]))
        def _():
            wgb[...] = wg_ref[0].astype(BF16)
            wub[...] = wu_ref[0].astype(BF16)
            wdb[...] = wd_ref[0].astype(BF16)

        x = _from_slabs(x_ref, 0, rows, s_n).astype(BF16)
        a = jnp.dot(x, wgb[...], preferred_element_type=F32)
        u = jnp.dot(x, wub[...], preferred_element_type=F32)
        hid = (a * jax.nn.sigmoid(a)) * u
        _to_slabs(y_ref, jnp.dot(hid.astype(BF16), wdb[...], preferred_element_type=F32))

    @pl.when(i >= nu_ref[0])
    def _():
        y_ref[...] = jnp.zeros_like(y_ref)


def _moe(xs2, tile_e, n_used, w_gate, w_up, w_down, layer):
    n_tiles = tile_e.shape[0]
    _, _, d, f = w_gate.shape
    s_n = d // LANES
    tile = lambda i, te, nu: (jnp.minimum(i, nu[0] - 1), 0)
    wspec = lambda shape: pl.BlockSpec((None, 1) + shape, lambda i, te, nu: (layer, te[i], 0, 0))
    return pl.pallas_call(
        _moe_kernel,
        grid_spec=pltpu.PrefetchScalarGridSpec(
            num_scalar_prefetch=2, grid=(n_tiles,),
            in_specs=[pl.BlockSpec((TM * s_n, LANES), tile), wspec((d, f)), wspec((d, f)), wspec((f, d))],
            out_specs=pl.BlockSpec((TM * s_n, LANES), lambda i, te, nu: (i, 0)),
            scratch_shapes=[pltpu.VMEM((d, f), BF16), pltpu.VMEM((d, f), BF16), pltpu.VMEM((f, d), BF16)],
        ),
        out_shape=jax.ShapeDtypeStruct((n_tiles * TM * s_n, LANES), F32),
        compiler_params=_cparams(1), name="moe_experts",
    )(tile_e, n_used, xs2, w_gate, w_up, w_down)


def _finish_kernel(pos_ref, h_ref, r_ref, ys_hbm, p_ref, pt_ref, wpg_ref, bpg_ref, wpp_ref, g1_ref, b1_ref, g2_ref,
                   b2_ref, o_ref, ybuf, sem, *, alpha, nt):
    i = pl.program_id(0)
    n = pl.num_programs(0)
    s_n = ys_hbm.shape[1]
    rows = h_ref.shape[0]

    @pl.when(i == 0)
    def _():
        for k in range(2):
            _start_row_gather(pos_ref, k * nt, rows, ys_hbm, ybuf.at[0], k * rows, s_n, sem.at[0])

    slot = i % 2
    pltpu.make_async_copy(ybuf.at[slot], ybuf.at[slot], sem.at[slot]).wait()
    y2 = ybuf.at[slot]
    r = r_ref[...]
    moe = r[:, 2:3] * _from_slabs(y2, 0, rows, s_n) + r[:, 3:4] * _from_slabs(y2, rows * s_n, rows, s_n)
    h2 = _ln(alpha * h_ref[...] + moe, g1_ref[...], b1_ref[...])
    h2b = h2.astype(BF16)
    pb = jnp.where(i == n - 1, pt_ref[...], p_ref[...]).astype(BF16)

    nxt = jnp.minimum(i + 1, n - 1)
    other = ybuf.at[1 - slot]
    per = 2 * rows // FINISH_CHUNKS
    cw = wpg_ref.shape[1] // FINISH_CHUNKS
    zs = []
    for c in range(FINISH_CHUNKS):
        for q in range(c * per, (c + 1) * per):
            k, rr = divmod(q, rows)
            row = pos_ref[k * nt + nxt * rows + rr]
            pltpu.make_async_copy(ys_hbm.at[row], other.at[pl.ds(q * s_n, s_n)], sem.at[1 - slot]).start()
        cs = slice(c * cw, (c + 1) * cw)
        zg = jnp.dot(h2b, wpg_ref[:, cs], preferred_element_type=F32) + bpg_ref[:, cs]
        zp = jnp.dot(pb, wpp_ref[:, cs], preferred_element_type=F32)
        zs.append(jax.nn.sigmoid(zg) * zp)
    z = jnp.concatenate(zs, axis=1)
    o_ref[...] = h2 + _ln(z, g2_ref[...], b2_ref[...])

    @pl.when(i == n - 1)
    def _():
        pltpu.make_async_copy(other, other, sem.at[1 - slot]).wait()


def _finish(pos, h, route, ys3, p, p_tail, w_pg, b_pg, w_pp, g1, b1, g2, b2, *, layer, alpha):
    nt, d = h.shape
    dp = p.shape[2]
    s_n = d // LANES
    last_prompt = nt // TM - 2
    kern = functools.partial(_finish_kernel, alpha=alpha, nt=nt)
    tile = lambda i, pos: (i, 0)
    const = _const_spec
    return pl.pallas_call(
        kern,
        grid_spec=pltpu.PrefetchScalarGridSpec(
            num_scalar_prefetch=1, grid=(nt // TM,),
            in_specs=[pl.BlockSpec((TM, d), tile), pl.BlockSpec((TM, LANES), tile), pl.BlockSpec(memory_space=pl.ANY),
                      pl.BlockSpec((None, TM, dp), lambda i, pos: (layer, jnp.minimum(i, last_prompt), 0)),
                      pl.BlockSpec((None, TM, dp), lambda i, pos: (layer, 0, 0)),
                      _layer_spec((d, d), (layer, 0, 0)), const((1, d)),
                      _layer_spec((dp, d), (layer, 0, 0)),
                      const((1, d)), const((1, d)), const((1, d)), const((1, d))],
            out_specs=pl.BlockSpec((TM, d), tile),
            scratch_shapes=[pltpu.VMEM((2, 2 * TM * s_n, LANES), F32), pltpu.SemaphoreType.DMA((2,))],
        ),
        out_shape=jax.ShapeDtypeStruct((nt, d), F32),
        compiler_params=_cparams(1), name="layer_finish",
    )(pos, h, route, ys3, p, p_tail, w_pg, b_pg, w_pp, g1, b1, g2, b2)


def _rope_tables(positions, hd, rot):
    half = rot // 2
    inv = ROPE_THETA ** (-jnp.arange(half, dtype=F32) / half)
    ang = positions.astype(F32)[:, None] * inv[None, :]
    cos, sin = jnp.cos(ang), jnp.sin(ang)
    n = positions.shape[0]
    ones = jnp.ones((n, hd - rot), F32)
    zeros_h = jnp.zeros((n, half), F32)
    zeros_r = jnp.zeros((n, hd - rot), F32)
    cos_t = jnp.concatenate([cos, cos, ones], axis=1)
    sa = jnp.concatenate([-sin, zeros_h, zeros_r], axis=1)
    sb = jnp.concatenate([zeros_h, sin, zeros_r], axis=1)
    return cos_t, sa, sb


def kernel(x_prompt, x_sample, cache_k, cache_v, page_table, p_prompt, p_sample, a_w_in, a_ln_g, a_ln_b, a_w_s, a_b_s, a_w_out, b_w_qkv, b_w_o, ln_g, ln_b, moe_w_rc, moe_b_rc, moe_w_rf, moe_b_rf, moe_w_gate, moe_w_up, moe_w_down, ple_w_proj, ple_w_gate, ple_b_gate):
    bp, tp, d = x_prompt.shape
    bs, ts, _ = x_sample.shape
    depth = ln_g.shape[0]
    _, page, nbl, n_heads, hd = cache_k.shape
    n_pages = page_table.shape[1]
    past_len = n_pages * page
    n_groups, chunk = a_w_s.shape[1], a_w_s.shape[2]
    da = a_w_out.shape[1]
    n_grp, epg = moe_w_rf.shape[2], moe_w_rf.shape[3]
    n_exp = n_grp * epg
    rot = hd // 4
    alpha = (2.0 * depth) ** 0.25
    np_rows, ns_rows = bp * tp, bs * ts
    nt = np_rows + TM
    s_n = d // LANES

    assert tp % MOBA_BLOCK == 0 and past_len % MOBA_BLOCK == 0 and MOBA_BLOCK % page == 0
    assert ts <= min(chunk, Q_PAD) and chunk % ts == 0 and ns_rows <= chunk and TM % chunk == 0
    assert n_heads * hd == d and hd == LANES and n_grp + n_exp <= LANES and n_pages // (MOBA_BLOCK // page) >= MOBA_TOPK

    def tail_tile(sample, lead=()):
        width = sample.shape[-1]
        pad = jnp.zeros(lead + (TM - ns_rows, width), sample.dtype)
        return jnp.concatenate([sample.reshape(lead + (ns_rows, width)), pad], axis=-2)

    x, x_tail = x_prompt.reshape(np_rows, d), tail_tile(x_sample)
    p_head, p_tail = p_prompt.reshape(depth, np_rows, -1), tail_tile(p_sample, (depth,))
    pt_flat = page_table.reshape(-1).astype(jnp.int32)

    positions = jnp.concatenate([jnp.tile(jnp.arange(tp, dtype=jnp.int32), bp),
                                 jnp.tile(past_len + jnp.arange(ts, dtype=jnp.int32), bs),
                                 jnp.zeros((TM - ns_rows,), jnp.int32)])
    tables = _rope_tables(positions, hd, rot)

    r = np.arange(chunk)
    tril = (r[None, :] <= r[:, None])
    same_seq = (r[None, :] // ts == r[:, None] // ts) & (r[:, None] < ns_rows) & (r[None, :] < ns_rows)
    msk2 = jnp.asarray(np.stack([tril, tril & same_seq]).astype(np.float32))

    km_cache = _cache_means(cache_k, pt_flat, bs=bs, n_pages=n_pages)
    n_sorted = 2 * nt + n_exp * TM

    w_in_b, w_out_b = a_w_in.astype(BF16), a_w_out.astype(BF16)
    w_qkv_b, w_o_b = b_w_qkv.astype(BF16), b_w_o.astype(BF16)
    w_pg_b, w_pp_b = ple_w_gate.astype(BF16), ple_w_proj.astype(BF16)

    np_tiles = np_rows // TM
    k_acc = v_acc = k_tl = v_tl = None
    chunk_rows = []
    for i in range(depth):
        j = i // 2
        wr = jnp.concatenate([moe_w_rc[i], moe_w_rf[i].reshape(d, n_exp),
                              jnp.zeros((d, LANES - n_grp - n_exp), F32)], axis=1).astype(BF16)
        br = jnp.concatenate([moe_b_rc[i], moe_b_rf[i].reshape(n_exp),
                              jnp.zeros((LANES - n_grp - n_exp,), F32)]).reshape(1, LANES)
        g0, b0 = ln_g[i, 0].reshape(1, d), ln_b[i, 0].reshape(1, d)
        if i % 2 == 0:
            ws_s = jnp.tile(a_w_s[j][:, :ts, :ts], (1, chunk // ts, chunk // ts))
            ws2 = jnp.stack([a_w_s[j], ws_s])
            bs_p = jnp.repeat(a_b_s[j].T, da // n_groups, axis=1)
            bs_s = jnp.repeat(jnp.tile(a_b_s[j][:, :ts], (1, chunk // ts)).T, da // n_groups, axis=1)
            bs2 = jnp.stack([bs_p, bs_s])
            h, hg, route, vn = _gmlp_layer(
                x, x_tail, w_in_b, a_ln_g[j].reshape(1, da), a_ln_b[j].reshape(1, da),
                ws2, msk2, bs2, w_out_b, g0, b0, wr, br, n_tiles=nt // TM, layer=j, alpha=alpha, n_grp=n_grp, epg=epg)
            chunk_rows.append(vn[:ns_rows].reshape(bs, ts, da))
        else:
            proj = functools.partial(_proj, x, w_qkv_b, layer=j, n_layers=nbl, n_heads=n_heads, rot=rot)
            head = dict(tile0=0, n_tiles=np_tiles)
            tail_t = dict(tile0=np_tiles, n_tiles=1)
            (q_t,) = proj(tables, None, part=0, rope=True, outs_wanted=("cols",), name="q_proj", **head)
            (q_tl,) = proj(tables, None, part=0, rope=True, outs_wanted=("cols",), name="q_proj_tail", **tail_t)
            k_acc, k_bf, km = proj(tables, k_acc, part=1, rope=True, outs_wanted=("slabs", "rows", "mean"),
                                   name="k_proj", **head)
            k_tl, k_bf_tl = proj(tables, k_tl, part=1, rope=True, outs_wanted=("slabs", "rows"),
                                 name="k_proj_tail", **tail_t)
            v_acc, v_t = proj(None, v_acc, part=2, rope=False, outs_wanted=("slabs", "cols"), name="v_proj", **head)
            (v_tl,) = proj(None, v_tl, part=2, rope=False, outs_wanted=("slabs",), name="v_proj_tail", **tail_t)
            attn = _moba_prompt(q_t, k_bf, v_t, km.reshape(np_tiles, d), bp=bp, tp=tp, n_heads=n_heads, hd=hd)

            def heads(a):
                a = a.reshape(bs, ts, n_heads, hd).transpose(0, 2, 1, 3).astype(BF16)
                return jnp.pad(a, ((0, 0), (0, 0), (0, Q_PAD - ts), (0, 0)))

            q_s, kn_s, vn_s = heads(q_tl[:, :ns_rows].T), heads(k_bf_tl[:ns_rows]), heads(v_tl[:ns_rows, j])
            ids = _sample_gate(q_s, km_cache[:, :, j].transpose(0, 2, 1, 3))
            ids_flat = ids[:, :, :ts, :MOBA_TOPK].reshape(-1)
            o_s = _sample_attn(pt_flat, ids_flat, cache_k, cache_v, q_s, kn_s, vn_s, layer=j, ts=ts, n_pages=n_pages)
            o_s = o_s[:, :, :ts].transpose(0, 2, 1, 3).reshape(ns_rows, d).astype(BF16)
            tail = jnp.concatenate([o_s, jnp.zeros((TM - ns_rows, d), BF16)], axis=0)
            h, hg, route = _oproj_layer(attn, tail, x, w_o_b, g0, b0, wr, br,
                                        layer=j, alpha=alpha, n_grp=n_grp, epg=epg)
        pos, tile_e, n_used, pend, npad = _moe_plan(route, n_exp)
        xs = _dispatch(hg, pos, pend, npad, n_sorted, s_n)
        ys = _moe(xs, tile_e, n_used, moe_w_gate, moe_w_up, moe_w_down, i)
        x = _finish(pos, h, route, ys.reshape(n_sorted, s_n, LANES), p_head, p_tail, w_pg_b,
                    ple_b_gate[i].reshape(1, d), w_pp_b,
                    ln_g[i, 1].reshape(1, d), ln_b[i, 1].reshape(1, d),
                    ln_g[i, 2].reshape(1, d), ln_b[i, 2].reshape(1, d), layer=i, alpha=alpha)
        x_tail = x[np_rows:]

    prompt_shape, sample_shape = (bp, tp, nbl, n_heads, hd), (bs, ts, nbl, n_heads, hd)
    y_prompt = x[:np_rows].reshape(bp, tp, d)
    y_sample = x[np_rows:np_rows + ns_rows].reshape(bs, ts, d)
    chunk_v = jnp.stack(chunk_rows, axis=2)
    return (y_prompt, y_sample, k_acc.reshape(prompt_shape), v_acc.reshape(prompt_shape),
            k_tl[:ns_rows].reshape(sample_shape), v_tl[:ns_rows].reshape(sample_shape), chunk_v)
```

```python
import functools

import numpy as np
import jax
import jax.numpy as jnp
from jax import lax
from jax.experimental import pallas as pl
from jax.experimental.pallas import tpu as pltpu

F32 = jnp.float32
BF16 = jnp.bfloat16

LANES = 128
TM = 256
MOBA_BLOCK = 256
MOBA_TOPK = 3
MOBA_HEADS_PER_STEP = 4
ROPE_THETA = 500000.0
LN_EPS = 1e-5
Q_PAD = 16
VMEM_LIMIT = 56 * 1024 * 1024
NEG_INF = float("-inf")
LOG2_E = 1.4426950408889634


def _cparams(n_axes):
    return pltpu.CompilerParams(dimension_semantics=("arbitrary",) * n_axes,
                                vmem_limit_bytes=VMEM_LIMIT)


def _const_spec(shape):
    zeros = (0,) * len(shape)
    return pl.BlockSpec(shape, lambda *a: zeros, pipeline_mode=pl.Buffered(1))


def _layer_spec(shape, idx):
    return pl.BlockSpec((None,) + shape, lambda *a: idx, pipeline_mode=pl.Buffered(1))


def _aligned(x, m):
    return x if isinstance(x, int) else pl.multiple_of(x, m)


def _ln(x, g, b):
    mu = jnp.mean(x, axis=-1, keepdims=True)
    xc = x - mu
    var = jnp.mean(xc * xc, axis=-1, keepdims=True)
    return xc * lax.rsqrt(var + LN_EPS) * g + b


def _to_slabs(ref, val):
    rows, d = val.shape
    s_n = d // LANES
    for s in range(s_n):
        ref[pl.ds(s, rows, stride=s_n), :] = val[:, s * LANES:(s + 1) * LANES]


def _from_slabs(ref, base, rows, s_n):
    cols = [ref[pl.ds(base + s, rows, stride=s_n), :] for s in range(s_n)]
    return jnp.concatenate(cols, axis=1)


def _route(h, wr_ref, br_ref, n_grp, epg):
    logits = jnp.dot(h.astype(BF16), wr_ref[...], preferred_element_type=F32) + br_ref[...]
    lane = lax.broadcasted_iota(jnp.int32, logits.shape, 1).astype(F32)
    big = float(LANES)
    lc = jnp.where(lane < n_grp, logits, NEG_INF)
    mc = jnp.max(lc, axis=1, keepdims=True)
    zc = jnp.sum(jnp.exp(lc - mc), axis=1, keepdims=True)
    pg = 1.0 / zc
    g = jnp.min(jnp.where(lc == mc, lane, big), axis=1, keepdims=True)
    lo = n_grp + g * epg
    lf = jnp.where((lane >= lo) & (lane < lo + epg), logits, NEG_INF)
    m1 = jnp.max(lf, axis=1, keepdims=True)
    i1 = jnp.min(jnp.where(lf == m1, lane, big), axis=1, keepdims=True)
    lf2 = jnp.where(lane == i1, NEG_INF, lf)
    m2 = jnp.max(lf2, axis=1, keepdims=True)
    i2 = jnp.min(jnp.where(lf2 == m2, lane, big), axis=1, keepdims=True)
    e2 = jnp.exp(m2 - m1)
    w1 = pg / (1.0 + e2)
    w2 = pg * e2 / (1.0 + e2)
    out = jnp.where(lane == 0, i1 - n_grp, 0.0)
    out = jnp.where(lane == 1, i2 - n_grp, out)
    out = jnp.where(lane == 2, w1, out)
    out = jnp.where(lane == 3, w2, out)
    return out


def _post_mix(x, mix, g_ref, b_ref, wr_ref, br_ref, h_ref, hg_ref, route_ref, alpha, n_grp, epg):
    h = _ln(alpha * x + mix, g_ref[...], b_ref[...])
    h_ref[...] = h
    _to_slabs(hg_ref, h)
    route_ref[...] = _route(h, wr_ref, br_ref, n_grp, epg)


def _gmlp_kernel(x_ref, xt_ref, win_ref, ag_ref, ab_ref, ws_ref, msk_ref, bs_ref, wout_ref,
                 g_ref, b_ref, wr_ref, br_ref,
                 h_ref, hg_ref, route_ref, vn_ref, gated_ref,
                 *, n_groups, chunk, alpha, n_grp, epg):
    is_tail = pl.program_id(0) == pl.num_programs(0) - 1
    x = jnp.where(is_tail, xt_ref[...], x_ref[...])
    z = jnp.dot(x.astype(BF16), win_ref[...], preferred_element_type=F32)
    z = jax.nn.gelu(z)
    da = z.shape[1] // 2
    u = z[:, :da]
    vn = _ln(z[:, da:], ag_ref[...], ab_ref[...])

    @pl.when(is_tail)
    def _():
        vn_ref[...] = vn

    vnb = vn.astype(BF16)
    gd = da // n_groups
    mask = msk_ref[0] > 0.0
    for g in range(n_groups):
        wsg = jnp.where(mask, ws_ref[0, g], 0.0).astype(BF16)
        for c in range(x.shape[0] // chunk):
            r0 = c * chunk
            s = jnp.dot(wsg, vnb[r0:r0 + chunk, g * gd:(g + 1) * gd], preferred_element_type=F32)
            s = s + bs_ref[0][:, g * gd:(g + 1) * gd]
            gated_ref[r0:r0 + chunk, g * gd:(g + 1) * gd] = (u[r0:r0 + chunk, g * gd:(g + 1) * gd] * s).astype(BF16)
    mix = jnp.dot(gated_ref[...], wout_ref[...], preferred_element_type=F32)
    _post_mix(x, mix, g_ref, b_ref, wr_ref, br_ref, h_ref, hg_ref, route_ref, alpha, n_grp, epg)


def _gmlp_layer(x, x_tail, w_in, a_g, a_b, ws2, msk2, bs2, w_out, g0, b0, wr, br, *, n_tiles, layer, alpha, n_grp, epg):
    d = x.shape[1]
    nt = n_tiles * TM
    da = w_out.shape[1]
    n_groups, chunk = ws2.shape[1], ws2.shape[2]
    last = n_tiles - 1
    s_n = d // LANES
    kern = functools.partial(_gmlp_kernel, n_groups=n_groups, chunk=chunk, alpha=alpha, n_grp=n_grp, epg=epg)
    sel = lambda i: (i // last, 0, 0, 0) if last > 0 else (0, 0, 0, 0)
    sel3 = lambda i: (i // last, 0, 0) if last > 0 else (0, 0, 0)
    return pl.pallas_call(
        kern,
        grid=(n_tiles,),
        in_specs=[
            pl.BlockSpec((TM, d), lambda i: (jnp.minimum(i, last - 1), 0)),
            pl.BlockSpec((TM, d), lambda i: (0, 0)),
            _layer_spec((d, 2 * da), (layer, 0, 0)),
            _const_spec((1, da)), _const_spec((1, da)),
            pl.BlockSpec((1, n_groups, chunk, chunk), sel),
            pl.BlockSpec((1, chunk, chunk), sel3),
            pl.BlockSpec((1, chunk, da), sel3),
            _layer_spec((da, d), (layer, 0, 0)),
            _const_spec((1, d)), _const_spec((1, d)),
            _const_spec((d, LANES)), _const_spec((1, LANES)),
        ],
        out_specs=[
            pl.BlockSpec((TM, d), lambda i: (i, 0)),
            pl.BlockSpec((TM * s_n, LANES), lambda i: (i, 0)),
            pl.BlockSpec((TM, LANES), lambda i: (i, 0)),
            pl.BlockSpec((TM, da), lambda i: (0, 0)),
        ],
        out_shape=[
            jax.ShapeDtypeStruct((nt, d), F32),
            jax.ShapeDtypeStruct((nt * s_n, LANES), F32),
            jax.ShapeDtypeStruct((nt, LANES), F32),
            jax.ShapeDtypeStruct((TM, da), F32),
        ],
        scratch_shapes=[pltpu.VMEM((TM, da), BF16)],
        compiler_params=_cparams(1),
        name="gmlp_mixer",
    )(x, x_tail, w_in, a_g, a_b, ws2, msk2, bs2, w_out, g0, b0, wr, br)


def _proj_kernel(*refs, rope, outs_wanted, n_heads, rot):
    x_ref, w_ref = refs[0], refs[1]
    pos = 2
    if rope:
        cos_ref, sa_ref, sb_ref = refs[2:5]
        pos = 5
    outs = dict(zip(outs_wanted, refs[pos:]))
    y = jnp.dot(x_ref[...].astype(BF16), w_ref[...], preferred_element_type=F32)
    if rope:
        hd = y.shape[1] // n_heads
        half = rot // 2
        cos, sa, sb = cos_ref[...], sa_ref[...], sb_ref[...]
        parts = []
        for h in range(n_heads):
            yh = y[:, h * hd:(h + 1) * hd]
            parts.append(yh * cos + pltpu.roll(yh, hd - half, 1) * sa + pltpu.roll(yh, half, 1) * sb)
        y = jnp.concatenate(parts, axis=1)
    if "slabs" in outs:
        ref = outs["slabs"]
        hd = y.shape[1] // n_heads
        if len(ref.shape) == 4:
            ref[:, 1:] = jnp.zeros((ref.shape[0], ref.shape[1] - 1) + ref.shape[2:], F32)
            for h in range(n_heads):
                ref[:, 0, h, :] = y[:, h * hd:(h + 1) * hd]
        else:
            for h in range(n_heads):
                ref[:, h, :] = y[:, h * hd:(h + 1) * hd]
    if "rows" in outs:
        outs["rows"][...] = y.astype(BF16)
    if "cols" in outs:
        outs["cols"][...] = y.T.astype(BF16)
    if "mean" in outs:
        outs["mean"][0] = jnp.mean(y, axis=0, keepdims=True)


def _proj(x, w, tables, slabs_in, *, layer, part, tile0, n_tiles, n_layers, rope, outs_wanted, n_heads, rot, name):
    d = x.shape[1]
    rows = n_tiles * TM
    hd = d // n_heads
    kern = functools.partial(_proj_kernel, rope=rope, outs_wanted=outs_wanted, n_heads=n_heads, rot=rot)
    in_specs = [pl.BlockSpec((TM, d), lambda i: (tile0 + i, 0)), _layer_spec((d, d), (layer, 0, part))]
    args = [x, w]
    if rope:
        in_specs += [pl.BlockSpec((TM, LANES), lambda i: (tile0 + i, 0))] * 3
        args += list(tables)
    aliases = {}
    if slabs_in is not None:
        aliases = {len(args): outs_wanted.index("slabs")}
        in_specs.append(pl.BlockSpec(memory_space=pl.ANY))
        args.append(slabs_in)
        kern = functools.partial(_drop_ref, kern, len(args) - 1)
        slab_spec = pl.BlockSpec((TM, None, n_heads, hd), lambda i: (i, layer, 0, 0))
    else:
        slab_spec = pl.BlockSpec((TM, n_layers, n_heads, hd), lambda i: (i, 0, 0, 0))
    specs = {
        "slabs": (slab_spec, jax.ShapeDtypeStruct((rows, n_layers, n_heads, hd), F32)),
        "rows": (pl.BlockSpec((TM, d), lambda i: (i, 0)), jax.ShapeDtypeStruct((rows, d), BF16)),
        "cols": (pl.BlockSpec((d, TM), lambda i: (0, i)), jax.ShapeDtypeStruct((d, rows), BF16)),
        "mean": (pl.BlockSpec((1, 1, d), lambda i: (i, 0, 0)), jax.ShapeDtypeStruct((n_tiles, 1, d), F32)),
    }
    return pl.pallas_call(
        kern, grid=(n_tiles,), in_specs=in_specs,
        out_specs=[specs[o][0] for o in outs_wanted], out_shape=[specs[o][1] for o in outs_wanted],
        input_output_aliases=aliases,
        compiler_params=_cparams(1), name=name,
    )(*args)


def _drop_ref(kern, at, *refs):
    return kern(*refs[:at], *refs[at + 1:])


def _top_blocks(gate, idx, topk, axis):
    big = float(gate.shape[axis])
    ids = []
    for _ in range(topk):
        m = jnp.max(gate, axis=axis, keepdims=True)
        sel = jnp.min(jnp.where(gate == m, idx, big), axis=axis, keepdims=True)
        sel = jnp.where(m > NEG_INF, sel, -1.0)
        ids.append(sel)
        gate = jnp.where(idx == sel, NEG_INF, gate)
    return ids


def _online_softmax(s, keep, m, l, c2):
    ms, ls, scales, ps = [], [], [], []
    for q0 in range(0, s.shape[1], LANES):
        qs = slice(q0, q0 + LANES)
        sh = jnp.where(keep[:, qs], s[:, qs], NEG_INF)
        m_new = jnp.maximum(m[:, qs], jnp.max(sh, axis=0, keepdims=True))
        a = jnp.exp2((m[:, qs] - m_new) * c2)
        p = jnp.exp2((sh - m_new) * c2)
        ms.append(m_new)
        ls.append(a * l[:, qs] + jnp.sum(p, axis=0, keepdims=True))
        scales.append(a)
        ps.append(p.astype(BF16))
    cat = lambda xs: jnp.concatenate(xs, axis=1)
    return cat(ms), cat(ls), cat(scales), cat(ps)


def _moba_prompt_kernel(qt_ref, k_ref, vt_ref, km_ref, o_ref, s_buf, p_buf, acc_buf, *, blk, topk, scale, hg, hd):
    n_blk = k_ref.shape[0] // blk
    kmb = km_ref[...].astype(BF16)
    key_i = lax.broadcasted_iota(jnp.int32, (blk, blk), 0)
    qry_i = lax.broadcasted_iota(jnp.int32, (blk, blk), 1)
    causal = key_i <= qry_i
    blk_i = lax.broadcasted_iota(jnp.int32, (n_blk, blk), 0).astype(F32)
    c2 = scale * LOG2_E

    def q_tile(qi, carry):
        q0 = _aligned(qi * blk, blk)
        qf = jnp.asarray(qi, F32)
        ids, init = [], []
        for h in range(hg):
            hs = slice(h * hd, (h + 1) * hd)
            qt = qt_ref[hs, pl.ds(q0, blk)]
            s_buf[h] = jnp.dot(k_ref[pl.ds(0, blk), hs], qt, preferred_element_type=F32)
            gate = jnp.dot(kmb[:, hs], qt, preferred_element_type=F32)
            gate = jnp.where(blk_i < qf, gate, NEG_INF)
            ids.append(_top_blocks(gate, blk_i, topk, 0))
            s = jnp.dot(k_ref[pl.ds(q0, blk), hs], qt, preferred_element_type=F32)
            m, l, _, p = _online_softmax(s, causal, jnp.full((1, blk), NEG_INF, F32), jnp.zeros((1, blk), F32), c2)
            p_buf[h] = p
            acc_buf[h] = jnp.zeros((hd, blk), F32)
            init.append((m, l, jnp.zeros((1, blk), F32)))

        def kv_step(n, c):
            prev0, stats = c
            prev0 = _aligned(prev0, blk)
            nf = jnp.asarray(n, F32)
            nxt0 = _aligned(jnp.minimum(n + 1, qi - 1) * blk, blk)
            out = []
            for h in range(hg):
                hs = slice(h * hd, (h + 1) * hd)
                m, l, a_prev = stats[h]
                s = s_buf[h]
                s_buf[h] = jnp.dot(k_ref[pl.ds(nxt0, blk), hs], qt_ref[hs, pl.ds(q0, blk)], preferred_element_type=F32)
                pv = jnp.dot(vt_ref[hs, pl.ds(prev0, blk)], p_buf[h], preferred_element_type=F32)
                acc_buf[h] = a_prev * acc_buf[h] + pv
                picked = ids[h][0] == nf
                for t in range(1, topk):
                    picked = picked | (ids[h][t] == nf)
                m, l, a, p = _online_softmax(s, picked, m, l, c2)
                p_buf[h] = p
                out.append((m, l, a))
            return _aligned(n * blk, blk), tuple(out)

        prev0, fin = lax.fori_loop(0, qi, kv_step, (jnp.asarray(q0, jnp.int32), tuple(init)))
        prev0 = _aligned(prev0, blk)
        for h in range(hg):
            hs = slice(h * hd, (h + 1) * hd)
            m, l, a_prev = fin[h]
            acc = a_prev * acc_buf[h] + jnp.dot(vt_ref[hs, pl.ds(prev0, blk)], p_buf[h], preferred_element_type=F32)
            o_ref[pl.ds(q0, blk), hs] = (acc / l).T.astype(o_ref.dtype)
        return carry

    lax.fori_loop(0, n_blk, q_tile, 0)


def _moba_prompt(qt, k, vt, km2, *, bp, tp, n_heads, hd):
    d = k.shape[1]
    hg = min(MOBA_HEADS_PER_STEP, n_heads)
    n_blk = tp // MOBA_BLOCK
    kern = functools.partial(_moba_prompt_kernel, blk=MOBA_BLOCK, topk=MOBA_TOPK, scale=hd ** -0.5, hg=hg, hd=hd)
    row_spec = pl.BlockSpec((tp, hg * hd), lambda b, g: (b, g))
    col_spec = pl.BlockSpec((hg * hd, tp), lambda b, g: (g, b))
    return pl.pallas_call(
        kern, grid=(bp, n_heads // hg),
        in_specs=[col_spec, row_spec, col_spec, pl.BlockSpec((n_blk, hg * hd), lambda b, g: (b, g))],
        out_specs=row_spec,
        out_shape=jax.ShapeDtypeStruct((bp * tp, d), BF16),
        scratch_shapes=[pltpu.VMEM((hg, MOBA_BLOCK, MOBA_BLOCK), F32), pltpu.VMEM((hg, MOBA_BLOCK, MOBA_BLOCK), BF16),
                        pltpu.VMEM((hg, hd, MOBA_BLOCK), F32)],
        compiler_params=_cparams(2), name="moba_prompt",
    )(qt, k, vt, km2)


def _cache_mean_kernel(pt_ref, *refs, ppb, bps, inv):
    o_ref = refs[ppb * bps]
    for blk in range(bps):
        acc = jnp.sum(refs[blk * ppb][...], axis=0)
        for t in range(1, ppb):
            acc = acc + jnp.sum(refs[blk * ppb + t][...], axis=0)
        o_ref[0, blk] = acc * inv


def _cache_means(cache_k, pt_flat, *, bs, n_pages):
    _, page, nbl, n_heads, hd = cache_k.shape
    ppb = MOBA_BLOCK // page
    n_cblk = n_pages // ppb
    bps = 2 if n_cblk % 2 == 0 else 1
    kern = functools.partial(_cache_mean_kernel, ppb=ppb, bps=bps, inv=1.0 / MOBA_BLOCK)

    def page_spec(t):
        return pl.BlockSpec((None, page, nbl, n_heads, hd),
                            lambda b, n, pt: (pt[b * n_pages + n * ppb * bps + t], 0, 0, 0, 0))

    return pl.pallas_call(
        kern,
        grid_spec=pltpu.PrefetchScalarGridSpec(
            num_scalar_prefetch=1, grid=(bs, n_cblk // bps),
            in_specs=[page_spec(t) for t in range(ppb * bps)],
            out_specs=pl.BlockSpec((1, bps, nbl, n_heads, hd), lambda b, n, pt: (b, n, 0, 0, 0)),
        ),
        out_shape=jax.ShapeDtypeStruct((bs, n_cblk, nbl, n_heads, hd), F32),
        compiler_params=_cparams(2), name="cache_block_means",
    )(pt_flat, *([cache_k] * (ppb * bps)))


def _sample_gate_kernel(q_ref, km_ref, o_ref, *, topk):
    n_heads = q_ref.shape[1]
    n_cblk = km_ref.shape[2]
    nt_dims = (((1,), (1,)), ((), ()))
    lane = lax.broadcasted_iota(jnp.int32, (Q_PAD, n_cblk), 1).astype(F32)
    out_lane = lax.broadcasted_iota(jnp.int32, (Q_PAD, LANES), 1)
    for h in range(n_heads):
        gate = lax.dot_general(q_ref[0, h], km_ref[0, h].astype(BF16), nt_dims, preferred_element_type=F32)
        ids = _top_blocks(gate, lane, topk, 1)
        out = jnp.zeros((Q_PAD, LANES), F32)
        for t in range(topk):
            out = jnp.where(out_lane == t, ids[t], out)
        o_ref[0, h] = out.astype(jnp.int32)


def _sample_gate(q_s, km_s):
    bs, n_heads, _, hd = q_s.shape
    n_cblk = km_s.shape[2]
    kern = functools.partial(_sample_gate_kernel, topk=MOBA_TOPK)
    return pl.pallas_call(
        kern, grid=(bs,),
        in_specs=[pl.BlockSpec((1, n_heads, Q_PAD, hd), lambda b: (b, 0, 0, 0)),
                  pl.BlockSpec((1, n_heads, n_cblk, hd), lambda b: (b, 0, 0, 0))],
        out_specs=pl.BlockSpec((1, n_heads, Q_PAD, LANES), lambda b: (b, 0, 0, 0)),
        out_shape=jax.ShapeDtypeStruct((bs, n_heads, Q_PAD, LANES), jnp.int32),
        compiler_params=_cparams(1), name="sample_gate",
    )(q_s, km_s)


def _sample_attn_kernel(pt_ref, ids_ref, ck_hbm, cv_hbm, q_ref, kn_ref, vn_ref, o_ref, kbuf, vbuf, sem,
                        *, layer, ts, topk, ppb, n_pages, scale):
    b, h = pl.program_id(0), pl.program_id(1)
    n_heads = pl.num_programs(1)
    step = b * n_heads + h
    n_steps = pl.num_programs(0) * n_heads
    page = kbuf.shape[2]

    def copies(bb, hh, slot):
        out = []
        for qi in range(ts):
            for t in range(topk):
                blk = ids_ref[((bb * n_heads + hh) * ts + qi) * topk + t]
                for half in range(ppb):
                    pg = pt_ref[bb * n_pages + blk * ppb + half]
                    j = (qi * topk + t) * ppb + half
                    out.append(pltpu.make_async_copy(ck_hbm.at[pg, :, layer, hh, :], kbuf.at[slot, j], sem.at[0, slot]))
                    out.append(pltpu.make_async_copy(cv_hbm.at[pg, :, layer, hh, :], vbuf.at[slot, j], sem.at[1, slot]))
        return out

    @pl.when(step == 0)
    def _():
        for ci, c in enumerate(copies(b, h, 0)):
            c.start(priority=ci % 2)

    @pl.when(step + 1 < n_steps)
    def _():
        nxt = step + 1
        for ci, c in enumerate(copies(nxt // n_heads, nxt % n_heads, nxt % 2)):
            c.start(priority=ci % 2)

    slot = step % 2
    for c in copies(b, h, slot):
        c.wait()

    nt_dims = (((1,), (1,)), ((), ()))
    q = q_ref[0, 0]
    rows = lax.broadcasted_iota(jnp.int32, (Q_PAD, Q_PAD), 0)
    cols = lax.broadcasted_iota(jnp.int32, (Q_PAD, Q_PAD), 1)
    s_new = lax.dot_general(q, kn_ref[0, 0], nt_dims, preferred_element_type=F32) * scale
    s_new = jnp.where((cols <= rows) & (cols < ts), s_new, NEG_INF)
    out_rows = lax.broadcasted_iota(jnp.int32, (Q_PAD, q.shape[1]), 0)
    out = jnp.zeros((Q_PAD, q.shape[1]), F32)
    per_q = topk * ppb
    for qi in range(ts):
        kq = kbuf[slot, qi * per_q:(qi + 1) * per_q].reshape(per_q * page, -1).astype(BF16)
        vq = vbuf[slot, qi * per_q:(qi + 1) * per_q].reshape(per_q * page, -1).astype(BF16)
        s = lax.dot_general(q, kq, nt_dims, preferred_element_type=F32) * scale
        m = jnp.maximum(jnp.max(s, axis=1, keepdims=True), jnp.max(s_new, axis=1, keepdims=True))
        p = jnp.exp(s - m)
        pn = jnp.exp(s_new - m)
        l = jnp.sum(p, axis=1, keepdims=True) + jnp.sum(pn, axis=1, keepdims=True)
        o = jnp.dot(p.astype(BF16), vq, preferred_element_type=F32)
        o = o + jnp.dot(pn.astype(BF16), vn_ref[0, 0], preferred_element_type=F32)
        out = jnp.where(out_rows == qi, o / l, out)
    o_ref[0, 0] = out


def _sample_attn(pt_flat, ids_flat, cache_k, cache_v, q_s, kn_s, vn_s, *, layer, ts, n_pages):
    bs, n_heads, _, hd = q_s.shape
    page = cache_k.shape[1]
    ppb = MOBA_BLOCK // page
    n_sel = ts * MOBA_TOPK * ppb
    kern = functools.partial(_sample_attn_kernel, layer=layer, ts=ts, topk=MOBA_TOPK, ppb=ppb,
                             n_pages=n_pages, scale=hd ** -0.5)
    qspec = pl.BlockSpec((1, 1, Q_PAD, hd), lambda b, h, pt, ids: (b, h, 0, 0))
    return pl.pallas_call(
        kern,
        grid_spec=pltpu.PrefetchScalarGridSpec(
            num_scalar_prefetch=2, grid=(bs, n_heads),
            in_specs=[pl.BlockSpec(memory_space=pl.ANY), pl.BlockSpec(memory_space=pl.ANY), qspec, qspec, qspec],
            out_specs=qspec,
            scratch_shapes=[pltpu.VMEM((2, n_sel, page, hd), F32), pltpu.VMEM((2, n_sel, page, hd), F32),
                            pltpu.SemaphoreType.DMA((2, 2))],
        ),
        out_shape=jax.ShapeDtypeStruct((bs, n_heads, Q_PAD, hd), F32),
        compiler_params=_cparams(2), name="sample_attn",
    )(pt_flat, ids_flat, cache_k, cache_v, q_s, kn_s, vn_s)


def _oproj_kernel(a_ref, tail_ref, x_ref, wo_ref, g_ref, b_ref, wr_ref, br_ref, h_ref, hg_ref, route_ref,
                  *, alpha, n_grp, epg):
    is_tail = pl.program_id(0) == pl.num_programs(0) - 1
    a = jnp.where(is_tail, tail_ref[...], a_ref[...])
    mix = jnp.dot(a, wo_ref[...], preferred_element_type=F32)
    _post_mix(x_ref[...], mix, g_ref, b_ref, wr_ref, br_ref, h_ref, hg_ref, route_ref, alpha, n_grp, epg)


def _oproj_layer(attn, tail, x, w_o, g0, b0, wr, br, *, layer, alpha, n_grp, epg):
    nt, d = x.shape
    s_n = d // LANES
    last_prompt = attn.shape[0] // TM - 1
    kern = functools.partial(_oproj_kernel, alpha=alpha, n_grp=n_grp, epg=epg)
    return pl.pallas_call(
        kern, grid=(nt // TM,),
        in_specs=[pl.BlockSpec((TM, d), lambda i: (jnp.minimum(i, last_prompt), 0)),
                  pl.BlockSpec((TM, d), lambda i: (0, 0)),
                  pl.BlockSpec((TM, d), lambda i: (i, 0)),
                  _layer_spec((d, d), (layer, 0, 0)), _const_spec((1, d)), _const_spec((1, d)),
                  _const_spec((d, LANES)), _const_spec((1, LANES))],
        out_specs=[pl.BlockSpec((TM, d), lambda i: (i, 0)),
                   pl.BlockSpec((TM * s_n, LANES), lambda i: (i, 0)),
                   pl.BlockSpec((TM, LANES), lambda i: (i, 0))],
        out_shape=[jax.ShapeDtypeStruct((nt, d), F32),
                   jax.ShapeDtypeStruct((nt * s_n, LANES), F32),
                   jax.ShapeDtypeStruct((nt, LANES), F32)],
        compiler_params=_cparams(1), name="attn_out_proj",
    )(attn, tail, x, w_o, g0, b0, wr, br)


GATHER_UNROLL = 16
DISPATCH_MAX_TILES = 3
FINISH_CHUNKS = 8


def _start_row_gather(idx_ref, idx_base, n_rows, src_hbm, dst, dst_base, s_n, sem):
    def body(rr, c):
        for k in range(GATHER_UNROLL):
            r = rr * GATHER_UNROLL + k
            row = idx_ref[idx_base + r]
            d0 = _aligned((dst_base + r) * s_n, s_n)
            pltpu.make_async_copy(src_hbm.at[row], dst.at[pl.ds(d0, s_n)], sem).start(priority=k % 2)
        return c
    lax.fori_loop(0, n_rows // GATHER_UNROLL, body, 0)


def _start_row_scatter(idx_ref, idx_base, n_rows, src, dst_hbm, s_n, sem):
    def body(rr, c):
        for k in range(GATHER_UNROLL):
            r = rr * GATHER_UNROLL + k
            s0 = _aligned(r * s_n, s_n)
            d0 = _aligned(idx_ref[idx_base + r] * s_n, s_n)
            pltpu.make_async_copy(src.at[pl.ds(s0, s_n)], dst_hbm.at[pl.ds(d0, s_n)], sem).start(priority=k % 2)
        return c
    lax.fori_loop(0, n_rows // GATHER_UNROLL, body, 0)


def _moe_plan(route, n_exp):
    nt = route.shape[0]
    e_flat = route[:, :2].astype(jnp.int32).T.reshape(-1)
    ar = jnp.arange(n_exp, dtype=jnp.int32)
    onehot = (e_flat[:, None] == ar[None, :]).astype(jnp.int32)
    csum = jnp.cumsum(onehot, axis=0)
    counts = csum[-1]
    padded = ((counts + TM - 1) // TM) * TM
    pend = jnp.cumsum(padded)
    pstart = pend - padded
    pos = jnp.sum(onehot * (csum - 1 + pstart[None, :]), axis=1)
    n_tiles = (2 * nt + n_exp * TM) // TM
    tile_start = jnp.arange(n_tiles, dtype=jnp.int32) * TM
    tile_e = jnp.sum((tile_start[:, None] >= pend[None, :]).astype(jnp.int32), axis=1)
    tile_e = jnp.minimum(tile_e, jnp.max(jnp.where(counts > 0, ar, 0)))
    n_used = (pend[-1] // TM).reshape(1)
    i32 = lambda a: a.astype(jnp.int32)
    return i32(pos), i32(tile_e), i32(n_used), i32(pend), i32(padded - counts)


def _dispatch_kernel(pos_ref, pend_ref, npad_ref, hg_ref, xs_hbm, zbuf, sem, zsem, *, nt, n_exp, s_n):
    i = pl.program_id(0)
    n_rows = xs_hbm.shape[0] // s_n
    rows = hg_ref.shape[0] // s_n

    def zero_rows(first, count):
        def body(r, c):
            d0 = _aligned((first + r) * s_n, s_n)
            pltpu.make_async_copy(zbuf, xs_hbm.at[pl.ds(d0, s_n)], zsem).start()
            return c
        lax.fori_loop(0, count, body, 0)

    @pl.when(i == 0)
    def _():
        zbuf[...] = jnp.zeros_like(zbuf)
        for e in range(n_exp):
            zero_rows(pend_ref[e] - npad_ref[e], npad_ref[e])
        zero_rows(pend_ref[n_exp - 1], n_rows - pend_ref[n_exp - 1])

    for k in range(2):
        _start_row_scatter(pos_ref, k * nt + i * rows, rows, hg_ref, xs_hbm, s_n, sem)
    for k in range(2):
        pltpu.make_async_copy(hg_ref, xs_hbm.at[pl.ds(0, rows * s_n)], sem).wait()

    @pl.when(i == pl.num_programs(0) - 1)
    def _():
        n_zero = n_exp * TM * s_n
        pltpu.make_async_copy(xs_hbm.at[pl.ds(0, n_zero)], xs_hbm.at[pl.ds(0, n_zero)], zsem).wait()


def _dispatch(hg, pos, pend, npad, n_rows, s_n):
    nt = hg.shape[0] // s_n
    n_exp = pend.shape[0]
    kern = functools.partial(_dispatch_kernel, nt=nt, n_exp=n_exp, s_n=s_n)
    tiles = max(k for k in range(1, DISPATCH_MAX_TILES + 1) if (nt // TM) % k == 0)
    rows = tiles * TM
    return pl.pallas_call(
        kern,
        grid_spec=pltpu.PrefetchScalarGridSpec(
            num_scalar_prefetch=3, grid=(nt // rows,),
            in_specs=[pl.BlockSpec((rows * s_n, LANES), lambda i, pos, pend, npad: (i, 0))],
            out_specs=pl.BlockSpec(memory_space=pl.ANY),
            scratch_shapes=[pltpu.VMEM((s_n, LANES), F32), pltpu.SemaphoreType.DMA(()), pltpu.SemaphoreType.DMA(())],
        ),
        out_shape=jax.ShapeDtypeStruct((n_rows * s_n, LANES), F32),
        compiler_params=_cparams(1), name="moe_dispatch",
    )(pos, pend, npad, hg)


def _moe_kernel(te_ref, nu_ref, x_ref, wg_ref, wu_ref, wd_ref, y_ref, wgb, wub, wdb):
    i = pl.program_id(0)
    s_n = wg_ref.shape[1] // LANES
    rows = x_ref.shape[0] // s_n

    @pl.when(i < nu_ref[0])
    def _():
        @pl.when((i == 0) | (te_ref[i] != te_ref[jnp.maximum(i - 1, 0)]))
        def _():
            wgb[...] = wg_ref[0].astype(BF16)
            wub[...] = wu_ref[0].astype(BF16)
            wdb[...] = wd_ref[0].astype(BF16)

        x = _from_slabs(x_ref, 0, rows, s_n).astype(BF16)
        a = jnp.dot(x, wgb[...], preferred_element_type=F32)
        u = jnp.dot(x, wub[...], preferred_element_type=F32)
        hid = (a * jax.nn.sigmoid(a)) * u
        _to_slabs(y_ref, jnp.dot(hid.astype(BF16), wdb[...], preferred_element_type=F32))

    @pl.when(i >= nu_ref[0])
    def _():
        y_ref[...] = jnp.zeros_like(y_ref)


def _moe(xs2, tile_e, n_used, w_gate, w_up, w_down, layer):
    n_tiles = tile_e.shape[0]
    _, _, d, f = w_gate.shape
    s_n = d // LANES
    tile = lambda i, te, nu: (jnp.minimum(i, nu[0] - 1), 0)
    wspec = lambda shape: pl.BlockSpec((None, 1) + shape, lambda i, te, nu: (layer, te[i], 0, 0))
    return pl.pallas_call(
        _moe_kernel,
        grid_spec=pltpu.PrefetchScalarGridSpec(
            num_scalar_prefetch=2, grid=(n_tiles,),
            in_specs=[pl.BlockSpec((TM * s_n, LANES), tile), wspec((d, f)), wspec((d, f)), wspec((f, d))],
            out_specs=pl.BlockSpec((TM * s_n, LANES), lambda i, te, nu: (i, 0)),
            scratch_shapes=[pltpu.VMEM((d, f), BF16), pltpu.VMEM((d, f), BF16), pltpu.VMEM((f, d), BF16)],
        ),
        out_shape=jax.ShapeDtypeStruct((n_tiles * TM * s_n, LANES), F32),
        compiler_params=_cparams(1), name="moe_experts",
    )(tile_e, n_used, xs2, w_gate, w_up, w_down)


def _finish_kernel(pos_ref, h_ref, r_ref, ys_hbm, p_ref, pt_ref, wpg_ref, bpg_ref, wpp_ref, g1_ref, b1_ref, g2_ref,
                   b2_ref, o_ref, ybuf, sem, *, alpha, nt):
    i = pl.program_id(0)
    n = pl.num_programs(0)
    s_n = ys_hbm.shape[1]
    rows = h_ref.shape[0]

    @pl.when(i == 0)
    def _():
        for k in range(2):
            _start_row_gather(pos_ref, k * nt, rows, ys_hbm, ybuf.at[0], k * rows, s_n, sem.at[0])

    slot = i % 2
    pltpu.make_async_copy(ybuf.at[slot], ybuf.at[slot], sem.at[slot]).wait()
    y2 = ybuf.at[slot]
    r = r_ref[...]
    moe = r[:, 2:3] * _from_slabs(y2, 0, rows, s_n) + r[:, 3:4] * _from_slabs(y2, rows * s_n, rows, s_n)
    h2 = _ln(alpha * h_ref[...] + moe, g1_ref[...], b1_ref[...])
    h2b = h2.astype(BF16)
    pb = jnp.where(i == n - 1, pt_ref[...], p_ref[...]).astype(BF16)

    nxt = jnp.minimum(i + 1, n - 1)
    other = ybuf.at[1 - slot]
    per = 2 * rows // FINISH_CHUNKS
    cw = wpg_ref.shape[1] // FINISH_CHUNKS
    zs = []
    for c in range(FINISH_CHUNKS):
        for q in range(c * per, (c + 1) * per):
            k, rr = divmod(q, rows)
            row = pos_ref[k * nt + nxt * rows + rr]
            pltpu.make_async_copy(ys_hbm.at[row], other.at[pl.ds(q * s_n, s_n)], sem.at[1 - slot]).start(priority=q % 2)
        cs = slice(c * cw, (c + 1) * cw)
        zg = jnp.dot(h2b, wpg_ref[:, cs], preferred_element_type=F32) + bpg_ref[:, cs]
        zp = jnp.dot(pb, wpp_ref[:, cs], preferred_element_type=F32)
        zs.append(jax.nn.sigmoid(zg) * zp)
    z = jnp.concatenate(zs, axis=1)
    o_ref[...] = h2 + _ln(z, g2_ref[...], b2_ref[...])

    @pl.when(i == n - 1)
    def _():
        pltpu.make_async_copy(other, other, sem.at[1 - slot]).wait()


def _finish(pos, h, route, ys3, p, p_tail, w_pg, b_pg, w_pp, g1, b1, g2, b2, *, layer, alpha):
    nt, d = h.shape
    dp = p.shape[2]
    s_n = d // LANES
    last_prompt = nt // TM - 2
    kern = functools.partial(_finish_kernel, alpha=alpha, nt=nt)
    tile = lambda i, pos: (i, 0)
    const = _const_spec
    return pl.pallas_call(
        kern,
        grid_spec=pltpu.PrefetchScalarGridSpec(
            num_scalar_prefetch=1, grid=(nt // TM,),
            in_specs=[pl.BlockSpec((TM, d), tile), pl.BlockSpec((TM, LANES), tile), pl.BlockSpec(memory_space=pl.ANY),
                      pl.BlockSpec((None, TM, dp), lambda i, pos: (layer, jnp.minimum(i, last_prompt), 0)),
                      pl.BlockSpec((None, TM, dp), lambda i, pos: (layer, 0, 0)),
                      _layer_spec((d, d), (layer, 0, 0)), const((1, d)),
                      _layer_spec((dp, d), (layer, 0, 0)),
                      const((1, d)), const((1, d)), const((1, d)), const((1, d))],
            out_specs=pl.BlockSpec((TM, d), tile),
            scratch_shapes=[pltpu.VMEM((2, 2 * TM * s_n, LANES), F32), pltpu.SemaphoreType.DMA((2,))],
        ),
        out_shape=jax.ShapeDtypeStruct((nt, d), F32),
        compiler_params=_cparams(1), name="layer_finish",
    )(pos, h, route, ys3, p, p_tail, w_pg, b_pg, w_pp, g1, b1, g2, b2)


def _rope_tables(positions, hd, rot):
    half = rot // 2
    inv = ROPE_THETA ** (-jnp.arange(half, dtype=F32) / half)
    ang = positions.astype(F32)[:, None] * inv[None, :]
    cos, sin = jnp.cos(ang), jnp.sin(ang)
    n = positions.shape[0]
    ones = jnp.ones((n, hd - rot), F32)
    zeros_h = jnp.zeros((n, half), F32)
    zeros_r = jnp.zeros((n, hd - rot), F32)
    cos_t = jnp.concatenate([cos, cos, ones], axis=1)
    sa = jnp.concatenate([-sin, zeros_h, zeros_r], axis=1)
    sb = jnp.concatenate([zeros_h, sin, zeros_r], axis=1)
    return cos_t, sa, sb


def kernel(x_prompt, x_sample, cache_k, cache_v, page_table, p_prompt, p_sample, a_w_in, a_ln_g, a_ln_b, a_w_s, a_b_s, a_w_out, b_w_qkv, b_w_o, ln_g, ln_b, moe_w_rc, moe_b_rc, moe_w_rf, moe_b_rf, moe_w_gate, moe_w_up, moe_w_down, ple_w_proj, ple_w_gate, ple_b_gate):
    bp, tp, d = x_prompt.shape
    bs, ts, _ = x_sample.shape
    depth = ln_g.shape[0]
    _, page, nbl, n_heads, hd = cache_k.shape
    n_pages = page_table.shape[1]
    past_len = n_pages * page
    n_groups, chunk = a_w_s.shape[1], a_w_s.shape[2]
    da = a_w_out.shape[1]
    n_grp, epg = moe_w_rf.shape[2], moe_w_rf.shape[3]
    n_exp = n_grp * epg
    rot = hd // 4
    alpha = (2.0 * depth) ** 0.25
    np_rows, ns_rows = bp * tp, bs * ts
    nt = np_rows + TM
    s_n = d // LANES

    assert tp % MOBA_BLOCK == 0 and past_len % MOBA_BLOCK == 0 and MOBA_BLOCK % page == 0
    assert ts <= min(chunk, Q_PAD) and chunk % ts == 0 and ns_rows <= chunk and TM % chunk == 0
    assert n_heads * hd == d and hd == LANES and n_grp + n_exp <= LANES and n_pages // (MOBA_BLOCK // page) >= MOBA_TOPK

    def tail_tile(sample, lead=()):
        width = sample.shape[-1]
        pad = jnp.zeros(lead + (TM - ns_rows, width), sample.dtype)
        return jnp.concatenate([sample.reshape(lead + (ns_rows, width)), pad], axis=-2)

    x, x_tail = x_prompt.reshape(np_rows, d), tail_tile(x_sample)
    p_head, p_tail = p_prompt.reshape(depth, np_rows, -1), tail_tile(p_sample, (depth,))
    pt_flat = page_table.reshape(-1).astype(jnp.int32)

    positions = jnp.concatenate([jnp.tile(jnp.arange(tp, dtype=jnp.int32), bp),
                                 jnp.tile(past_len + jnp.arange(ts, dtype=jnp.int32), bs),
                                 jnp.zeros((TM - ns_rows,), jnp.int32)])
    tables = _rope_tables(positions, hd, rot)

    r = np.arange(chunk)
    tril = (r[None, :] <= r[:, None])
    same_seq = (r[None, :] // ts == r[:, None] // ts) & (r[:, None] < ns_rows) & (r[None, :] < ns_rows)
    msk2 = jnp.asarray(np.stack([tril, tril & same_seq]).astype(np.float32))

    km_cache = _cache_means(cache_k, pt_flat, bs=bs, n_pages=n_pages)
    n_sorted = 2 * nt + n_exp * TM

    w_in_b, w_out_b = a_w_in.astype(BF16), a_w_out.astype(BF16)
    w_qkv_b, w_o_b = b_w_qkv.astype(BF16), b_w_o.astype(BF16)
    w_pg_b, w_pp_b = ple_w_gate.astype(BF16), ple_w_proj.astype(BF16)

    np_tiles = np_rows // TM
    k_acc = v_acc = k_tl = v_tl = None
    chunk_rows = []
    for i in range(depth):
        j = i // 2
        wr = jnp.concatenate([moe_w_rc[i], moe_w_rf[i].reshape(d, n_exp),
                              jnp.zeros((d, LANES - n_grp - n_exp), F32)], axis=1).astype(BF16)
        br = jnp.concatenate([moe_b_rc[i], moe_b_rf[i].reshape(n_exp),
                              jnp.zeros((LANES - n_grp - n_exp,), F32)]).reshape(1, LANES)
        g0, b0 = ln_g[i, 0].reshape(1, d), ln_b[i, 0].reshape(1, d)
        if i % 2 == 0:
            ws_s = jnp.tile(a_w_s[j][:, :ts, :ts], (1, chunk // ts, chunk // ts))
            ws2 = jnp.stack([a_w_s[j], ws_s])
            bs_p = jnp.repeat(a_b_s[j].T, da // n_groups, axis=1)
            bs_s = jnp.repeat(jnp.tile(a_b_s[j][:, :ts], (1, chunk // ts)).T, da // n_groups, axis=1)
            bs2 = jnp.stack([bs_p, bs_s])
            h, hg, route, vn = _gmlp_layer(
                x, x_tail, w_in_b, a_ln_g[j].reshape(1, da), a_ln_b[j].reshape(1, da),
                ws2, msk2, bs2, w_out_b, g0, b0, wr, br, n_tiles=nt // TM, layer=j, alpha=alpha, n_grp=n_grp, epg=epg)
            chunk_rows.append(vn[:ns_rows].reshape(bs, ts, da))
        else:
            proj = functools.partial(_proj, x, w_qkv_b, layer=j, n_layers=nbl, n_heads=n_heads, rot=rot)
            head = dict(tile0=0, n_tiles=np_tiles)
            tail_t = dict(tile0=np_tiles, n_tiles=1)
            (q_t,) = proj(tables, None, part=0, rope=True, outs_wanted=("cols",), name="q_proj", **head)
            (q_tl,) = proj(tables, None, part=0, rope=True, outs_wanted=("cols",), name="q_proj_tail", **tail_t)
            k_acc, k_bf, km = proj(tables, k_acc, part=1, rope=True, outs_wanted=("slabs", "rows", "mean"),
                                   name="k_proj", **head)
            k_tl, k_bf_tl = proj(tables, k_tl, part=1, rope=True, outs_wanted=("slabs", "rows"),
                                 name="k_proj_tail", **tail_t)
            v_acc, v_t = proj(None, v_acc, part=2, rope=False, outs_wanted=("slabs", "cols"), name="v_proj", **head)
            (v_tl,) = proj(None, v_tl, part=2, rope=False, outs_wanted=("slabs",), name="v_proj_tail", **tail_t)
            attn = _moba_prompt(q_t, k_bf, v_t, km.reshape(np_tiles, d), bp=bp, tp=tp, n_heads=n_heads, hd=hd)

            def heads(a):
                a = a.reshape(bs, ts, n_heads, hd).transpose(0, 2, 1, 3).astype(BF16)
                return jnp.pad(a, ((0, 0), (0, 0), (0, Q_PAD - ts), (0, 0)))

            q_s, kn_s, vn_s = heads(q_tl[:, :ns_rows].T), heads(k_bf_tl[:ns_rows]), heads(v_tl[:ns_rows, j])
            ids = _sample_gate(q_s, km_cache[:, :, j].transpose(0, 2, 1, 3))
            ids_flat = ids[:, :, :ts, :MOBA_TOPK].reshape(-1)
            o_s = _sample_attn(pt_flat, ids_flat, cache_k, cache_v, q_s, kn_s, vn_s, layer=j, ts=ts, n_pages=n_pages)
            o_s = o_s[:, :, :ts].transpose(0, 2, 1, 3).reshape(ns_rows, d).astype(BF16)
            tail = jnp.concatenate([o_s, jnp.zeros((TM - ns_rows, d), BF16)], axis=0)
            h, hg, route = _oproj_layer(attn, tail, x, w_o_b, g0, b0, wr, br,
                                        layer=j, alpha=alpha, n_grp=n_grp, epg=epg)
        pos, tile_e, n_used, pend, npad = _moe_plan(route, n_exp)
        xs = _dispatch(hg, pos, pend, npad, n_sorted, s_n)
        ys = _moe(xs, tile_e, n_used, moe_w_gate, moe_w_up, moe_w_down, i)
        x = _finish(pos, h, route, ys.reshape(n_sorted, s_n, LANES), p_head, p_tail, w_pg_b,
                    ple_b_gate[i].reshape(1, d), w_pp_b,
                    ln_g[i, 1].reshape(1, d), ln_b[i, 1].reshape(1, d),
                    ln_g[i, 2].reshape(1, d), ln_b[i, 2].reshape(1, d), layer=i, alpha=alpha)
        x_tail = x[np_rows:]

    prompt_shape, sample_shape = (bp, tp, nbl, n_heads, hd), (bs, ts, nbl, n_heads, hd)
    y_prompt = x[:np_rows].reshape(bp, tp, d)
    y_sample = x[np_rows:np_rows + ns_rows].reshape(bs, ts, d)
    chunk_v = jnp.stack(chunk_rows, axis=2)
    return (y_prompt, y_sample, k_acc.reshape(prompt_shape), v_acc.reshape(prompt_shape),
            k_tl[:ns_rows].reshape(sample_shape), v_tl[:ns_rows].reshape(sample_shape), chunk_v)
```
